```python
import jax, jax.numpy as jnp
from jax import lax
import numpy as np

D_MODEL = 1024
BATCH = 8
SEQ = 2048
DEPTH = 4
DEC_BATCH = 32
DEC_SEQ = 4
PAST_LEN = 16384
PAGE_SIZE = 128

N_MIXERS = 3
N_MLA = (DEPTH + 2) // 3
N_SB = (DEPTH + 1) // 3
N_HG = DEPTH // 3
MLA_HEADS = 8
QK_NOPE = 128
QK_ROPE = 64
V_HEAD = 128
Q_LORA = 256
KV_LORA = 256
QK_HEAD = QK_NOPE + QK_ROPE
MLA_ROW = KV_LORA + QK_ROPE
MLA_SCALE = QK_HEAD ** -0.5
ROPE_BASE = 10000.0
SB_HEADS = 8
SB_HEAD = 128
SB_SCALE = SB_HEAD ** -0.5
HG_HEADS = 8
HG_DK = 128
HG_DV = 128
HG_K = HG_HEADS * HG_DK
HG_V = HG_HEADS * HG_DV
HG_CHUNK = 32
D_FF = 2816
CONV_W = 3
Q_BLOCK = 128
EPS = 1e-6
N_MOD = 6

kernel_name = 'mla_stickbreak_hgrn2_convffn_step'


def rmsnorm(x, g):
    xf = x.astype(jnp.float32)
    y = xf * lax.rsqrt(jnp.mean(xf * xf, axis=-1, keepdims=True) + EPS)
    return (y * g.astype(jnp.float32)).astype(x.dtype)


def ada_terms(c, w_mod, b_mod):
    m = (jax.nn.silu(c) @ w_mod + b_mod)[:, None, :]
    return jnp.split(m, N_MOD, axis=-1)


def rope(x, pos):
    half = x.shape[-1] // 2
    inv = ROPE_BASE ** (-jnp.arange(half, dtype=jnp.float32) / half)
    ang = pos.astype(jnp.float32)[:, None] * inv[None, :]
    ang = ang.reshape(ang.shape[:1] + (1,) * (x.ndim - 3) + (half,))
    cos, sin = jnp.cos(ang), jnp.sin(ang)
    xf = x.astype(jnp.float32)
    x1, x2 = xf[..., :half], xf[..., half:]
    return jnp.concatenate([x1 * cos - x2 * sin, x1 * sin + x2 * cos], axis=-1).astype(x.dtype)


def mla_project(h, pos, w_a, g_qa, g_kva, w_uq, g_q):
    b, t, _ = h.shape
    q_lat, c_kv, k_pe = jnp.split(h @ w_a, [Q_LORA, Q_LORA + KV_LORA], axis=-1)
    q = (rmsnorm(q_lat, g_qa) @ w_uq).reshape(b, t, MLA_HEADS, QK_HEAD)
    q = jnp.concatenate([q[..., :QK_NOPE], rope(q[..., QK_NOPE:], pos)], axis=-1)
    q = rmsnorm(q, g_q)
    rows = jnp.concatenate([rmsnorm(c_kv, g_kva), rope(k_pe, pos)], axis=-1)
    return q, rows


def mla_keys(rows, w_ukv, g_k):
    kv = (rows[..., :KV_LORA] @ w_ukv).reshape(rows.shape[:-1] + (MLA_HEADS, QK_NOPE + V_HEAD))
    k_pe = jnp.broadcast_to(rows[..., None, KV_LORA:], rows.shape[:-1] + (MLA_HEADS, QK_ROPE))
    k = rmsnorm(jnp.concatenate([kv[..., :QK_NOPE], k_pe], axis=-1), g_k)
    return k, kv[..., QK_NOPE:]


def mla_prompt(q, k, v):
    s_len = q.shape[1]
    outs = []
    for blk in range(s_len // Q_BLOCK):
        lo, hi = blk * Q_BLOCK, (blk + 1) * Q_BLOCK
        s = jnp.einsum('bqhd,bkhd->bhqk', q[:, lo:hi].astype(jnp.float32), k[:, :hi].astype(jnp.float32)) * MLA_SCALE
        mask = (lo + jnp.arange(Q_BLOCK))[:, None] >= jnp.arange(hi)[None, :]
        p = jax.nn.softmax(jnp.where(mask, s, -jnp.inf), axis=-1)
        outs.append(jnp.einsum('bhqk,bkhd->bqhd', p, v[:, :hi].astype(jnp.float32)))
    return jnp.concatenate(outs, axis=1)


def mla_decode(q, rows_new, cache, layer, page_table, w_ukv, g_k):
    t = q.shape[1]
    qf = q.astype(jnp.float32)
    k_new, v_new = mla_keys(rows_new, w_ukv, g_k)
    s = jnp.einsum('bqhd,bkhd->bhqk', qf, k_new.astype(jnp.float32)) * MLA_SCALE
    s = jnp.where(jnp.arange(t)[:, None] >= jnp.arange(t)[None, :], s, -jnp.inf)
    m = s.max(-1)
    p = jnp.exp(s - m[..., None])
    carry = (m, p.sum(-1), jnp.einsum('bhqk,bkhd->bhqd', p, v_new.astype(jnp.float32)))

    def step(carry, pt):
        m, l, acc = carry
        k, v = mla_keys(cache[layer, pt], w_ukv, g_k)
        s = jnp.einsum('bqhd,bkhd->bhqk', qf, k.astype(jnp.float32)) * MLA_SCALE
        m_new = jnp.maximum(m, s.max(-1))
        alpha = jnp.exp(m - m_new)
        p = jnp.exp(s - m_new[..., None])
        acc = acc * alpha[..., None] + jnp.einsum('bhqk,bkhd->bhqd', p, v.astype(jnp.float32))
        return (m_new, l * alpha + p.sum(-1), acc), None

    (m, l, acc), _ = lax.scan(step, carry, page_table.T)
    return (acc / l[..., None]).transpose(0, 2, 1, 3)


def sb_project(h, w_qkv):
    b, t, _ = h.shape
    qkv = (h @ w_qkv).reshape(b, t, 3, SB_HEADS, SB_HEAD)
    return qkv[:, :, 0], qkv[:, :, 1], qkv[:, :, 2]


def sb_weights(z, mask, later_keep):
    log_keep = jnp.where(mask, jax.nn.log_sigmoid(-z), 0.0)
    later = lax.cumsum(log_keep, axis=z.ndim - 1, reverse=True) - log_keep + later_keep
    a = jnp.where(mask, jnp.exp(jax.nn.log_sigmoid(z) + later), 0.0)
    return a, log_keep.sum(-1)


def sb_prompt(q, k, v):
    s_len = q.shape[1]
    outs = []
    for blk in range(s_len // Q_BLOCK):
        lo, hi = blk * Q_BLOCK, (blk + 1) * Q_BLOCK
        z = jnp.einsum('bqhd,bkhd->bhqk', q[:, lo:hi].astype(jnp.float32), k[:, :hi].astype(jnp.float32)) * SB_SCALE
        mask = (lo + jnp.arange(Q_BLOCK))[:, None] > jnp.arange(hi)[None, :]
        a, _ = sb_weights(z, mask, 0.0)
        outs.append(jnp.einsum('bhqk,bkhd->bqhd', a, v[:, :hi].astype(jnp.float32)))
    return jnp.concatenate(outs, axis=1)


def sb_decode(q, k_new, v_new, cache, layer, page_table):
    t = q.shape[1]
    qf = q.astype(jnp.float32)
    z = jnp.einsum('bqhd,bkhd->bhqk', qf, k_new.astype(jnp.float32)) * SB_SCALE
    a, keep = sb_weights(z, jnp.arange(t)[:, None] > jnp.arange(t)[None, :], 0.0)
    acc = jnp.einsum('bhqk,bkhd->bhqd', a, v_new.astype(jnp.float32))

    def step(carry, pt):
        acc, keep_later = carry
        kv = cache[layer, pt].astype(jnp.float32)
        z = jnp.einsum('bqhd,bkhd->bhqk', qf, kv[:, :, 0]) * SB_SCALE
        a, keep = sb_weights(z, True, keep_later[..., None])
        return (acc + jnp.einsum('bhqk,bkhd->bhqd', a, kv[:, :, 1]), keep_later + keep), None

    (acc, _), _ = lax.scan(step, (acc, keep), page_table.T, reverse=True)
    return acc.transpose(0, 2, 1, 3)


def hgrn_inputs(h, w_in, lb):
    b, t, _ = h.shape
    q, f, i, g = jnp.split((h @ w_in).astype(jnp.float32), [HG_K, 2 * HG_K, 2 * HG_K + HG_V], axis=-1)
    forget = lb + (1.0 - lb) * jax.nn.sigmoid(f)

    def heads(z, d):
        return z.reshape(b, t, HG_HEADS, d).transpose(0, 2, 1, 3)

    return heads(q, HG_DK), heads(1.0 - forget, HG_DK), heads(i, HG_DV), heads(jnp.log(forget), HG_DK), g


def gla_chunk(state, q, k, v, log_f):
    c = q.shape[2]
    cum = jnp.cumsum(log_f, axis=2)
    q_dec = q * jnp.exp(cum)
    k_dec = k * jnp.exp(-cum)
    causal = jnp.arange(c)[:, None] >= jnp.arange(c)[None, :]
    att = jnp.where(causal, jnp.einsum('bhtk,bhsk->bhts', q_dec, k_dec), 0.0)
    out = jnp.einsum('bhtk,bhkv->bhtv', q_dec, state) + jnp.einsum('bhts,bhsv->bhtv', att, v)
    last = cum[:, :, -1:, :]
    new_state = state * jnp.exp(last[:, :, 0, :, None]) + jnp.einsum('bhsk,bhsv->bhkv', k * jnp.exp(last - cum), v)
    return new_state, out


def hgrn_prompt(q, k, v, log_f):
    b, h, s_len, _ = q.shape
    n = s_len // HG_CHUNK

    def chunks(z):
        return z.reshape(b, h, n, HG_CHUNK, z.shape[-1]).transpose(2, 0, 1, 3, 4)

    state0 = jnp.zeros((b, h, HG_DK, HG_DV), jnp.float32)
    final, out = lax.scan(lambda st, xs: gla_chunk(st, *xs), state0,
                          (chunks(q), chunks(k), chunks(v), chunks(log_f)))
    return out.transpose(1, 2, 0, 3, 4).reshape(b, h, s_len, HG_DV), final


def hgrn_out(o, g, g_o, w_o, dtype):
    b, h, t, _ = o.shape
    o = rmsnorm(o.transpose(0, 2, 1, 3), g_o) * jax.nn.silu(g).reshape(b, t, h, HG_DV)
    return o.reshape(b, t, h * HG_DV).astype(dtype) @ w_o


def conv_ffn(h, past, w_up, conv_w, conv_b, w_down):
    t = h.shape[1]
    a, u = jnp.split(h @ w_up, 2, axis=-1)
    ext = jnp.concatenate([past.astype(a.dtype), a], axis=1)
    conv = conv_b + ext[:, 0:t] * conv_w[0]
    for j in range(1, CONV_W):
        conv = conv + ext[:, j:j + t] * conv_w[j]
    y = (jax.nn.silu(conv) * u) @ w_down
    return y, ext[:, -(CONV_W - 1):]


def setup_inputs(seed: int = 0) -> dict:
    key = jax.random.key(seed)
    ks = iter(jax.random.split(key, 40))

    def nrm(shape, scale):
        return jax.random.normal(next(ks), shape, jnp.float32) * scale

    def gain(shape):
        return 1.0 + nrm(shape, 0.05)

    n_pages = PAST_LEN // PAGE_SIZE
    n_used = DEC_BATCH * n_pages
    n_pool = n_used + n_used // 4
    perm = jax.random.permutation(next(ks), n_pool)
    page_table = perm[:n_used].reshape(DEC_BATCH, n_pages).astype(jnp.int32)
    d = D_MODEL
    return {
        'x_prompt': nrm((BATCH, SEQ, d), 1.0),
        'x_sample': nrm((DEC_BATCH, DEC_SEQ, d), 1.0),
        'c_prompt': nrm((BATCH, d), 1.0),
        'c_sample': nrm((DEC_BATCH, d), 1.0),
        'page_table': page_table,
        'cache_mla': nrm((N_MLA, n_pool, PAGE_SIZE, MLA_ROW), 1.0),
        'cache_sb_kv': nrm((N_SB, n_pool, PAGE_SIZE, 2, SB_HEADS, SB_HEAD), 1.0),
        'state_hgrn': nrm((N_HG, DEC_BATCH, HG_HEADS, HG_DK, HG_DV), 0.5),
        'state_ffn_conv': nrm((DEPTH, DEC_BATCH, CONV_W - 1, D_FF), 1.0),
        'w_mod': nrm((DEPTH, d, N_MOD * d), d ** -0.5),
        'b_mod': nrm((DEPTH, N_MOD * d), 0.02),
        'norm_g': gain((DEPTH, 2, d)),
        'w_mla_a': nrm((N_MLA, d, Q_LORA + KV_LORA + QK_ROPE), d ** -0.5),
        'g_mla_qa': gain((N_MLA, Q_LORA)),
        'g_mla_kva': gain((N_MLA, KV_LORA)),
        'w_mla_uq': nrm((N_MLA, Q_LORA, MLA_HEADS * QK_HEAD), Q_LORA ** -0.5),
        'w_mla_ukv': nrm((N_MLA, KV_LORA, MLA_HEADS * (QK_NOPE + V_HEAD)), KV_LORA ** -0.5),
        'g_mla_q': gain((N_MLA, QK_HEAD)),
        'g_mla_k': gain((N_MLA, QK_HEAD)),
        'w_mla_o': nrm((N_MLA, MLA_HEADS * V_HEAD, d), (MLA_HEADS * V_HEAD) ** -0.5),
        'w_sb_qkv': nrm((N_SB, d, 3 * SB_HEADS * SB_HEAD), d ** -0.5),
        'w_sb_o': nrm((N_SB, SB_HEADS * SB_HEAD, d), (SB_HEADS * SB_HEAD) ** -0.5),
        'w_hg_in': nrm((N_HG, d, 2 * HG_K + 2 * HG_V), d ** -0.5),
        'hg_lb_logits': nrm((DEPTH, HG_K), 0.5),
        'g_hg_o': gain((N_HG, HG_DV)),
        'w_hg_o': nrm((N_HG, HG_V, d), HG_V ** -0.5),
        'w_ffn_up': nrm((DEPTH, d, 2 * D_FF), d ** -0.5),
        'ffn_conv_w': nrm((DEPTH, CONV_W, D_FF), CONV_W ** -0.5),
        'ffn_conv_b': nrm((DEPTH, D_FF), 0.02),
        'w_ffn_down': nrm((DEPTH, D_FF, d), D_FF ** -0.5),
    }


def reference(x_prompt, x_sample, c_prompt, c_sample, page_table, cache_mla, cache_sb_kv, state_hgrn,
              state_ffn_conv, w_mod, b_mod, norm_g, w_mla_a, g_mla_qa, g_mla_kva, w_mla_uq, w_mla_ukv,
              g_mla_q, g_mla_k, w_mla_o, w_sb_qkv, w_sb_o, w_hg_in, hg_lb_logits, g_hg_o, w_hg_o,
              w_ffn_up, ffn_conv_w, ffn_conv_b, w_ffn_down):
    b, s_len, _ = x_prompt.shape
    db, t, _ = x_sample.shape
    past = page_table.shape[1] * PAGE_SIZE
    pos_p = jnp.arange(s_len)
    pos_s = past + jnp.arange(t)
    lb_sm = jax.nn.softmax(hg_lb_logits.astype(jnp.float32), axis=0)
    lower_bounds = jnp.cumsum(lb_sm, axis=0) - lb_sm[0]

    xp, xs = x_prompt, x_sample
    mla_p, mla_s, sb_p, sb_s, hg_p, hg_s, ffn_p, ffn_s = [], [], [], [], [], [], [], []
    for i in range(DEPTH):
        mp = ada_terms(c_prompt, w_mod[i], b_mod[i])
        ms = ada_terms(c_sample, w_mod[i], b_mod[i])
        hp = rmsnorm(xp, norm_g[i, 0]) * (1.0 + mp[1]) + mp[0]
        hs = rmsnorm(xs, norm_g[i, 0]) * (1.0 + ms[1]) + ms[0]
        kind, j = i % N_MIXERS, i // N_MIXERS
        if kind == 0:
            qp, rows_p = mla_project(hp, pos_p, w_mla_a[j], g_mla_qa[j], g_mla_kva[j], w_mla_uq[j], g_mla_q[j])
            kp, vp = mla_keys(rows_p, w_mla_ukv[j], g_mla_k[j])
            op = mla_prompt(qp, kp, vp)
            qs, rows_s = mla_project(hs, pos_s, w_mla_a[j], g_mla_qa[j], g_mla_kva[j], w_mla_uq[j], g_mla_q[j])
            os_ = mla_decode(qs, rows_s, cache_mla, j, page_table, w_mla_ukv[j], g_mla_k[j])
            yp = op.reshape(b, s_len, -1).astype(xp.dtype) @ w_mla_o[j]
            ys = os_.reshape(db, t, -1).astype(xs.dtype) @ w_mla_o[j]
            mla_p.append(rows_p)
            mla_s.append(rows_s)
        elif kind == 1:
            qp, kp, vp = sb_project(hp, w_sb_qkv[j])
            op = sb_prompt(qp, kp, vp)
            qs, k_s, v_s = sb_project(hs, w_sb_qkv[j])
            os_ = sb_decode(qs, k_s, v_s, cache_sb_kv, j, page_table)
            yp = op.reshape(b, s_len, -1).astype(xp.dtype) @ w_sb_o[j]
            ys = os_.reshape(db, t, -1).astype(xs.dtype) @ w_sb_o[j]
            sb_p.append(jnp.stack([kp, vp], axis=2))
            sb_s.append(jnp.stack([k_s, v_s], axis=2))
        else:
            qp, kp, vp, lfp, gp = hgrn_inputs(hp, w_hg_in[j], lower_bounds[i])
            op, st_p = hgrn_prompt(qp, kp, vp, lfp)
            qs, k_s, v_s, lfs, gs = hgrn_inputs(hs, w_hg_in[j], lower_bounds[i])
            st_s, os_ = gla_chunk(state_hgrn[j].astype(jnp.float32), qs, k_s, v_s, lfs)
            yp = hgrn_out(op, gp, g_hg_o[j], w_hg_o[j], xp.dtype)
            ys = hgrn_out(os_, gs, g_hg_o[j], w_hg_o[j], xs.dtype)
            hg_p.append(st_p)
            hg_s.append(st_s)
        xp = xp + mp[2] * yp
        xs = xs + ms[2] * ys
        hp = rmsnorm(xp, norm_g[i, 1]) * (1.0 + mp[4]) + mp[3]
        hs = rmsnorm(xs, norm_g[i, 1]) * (1.0 + ms[4]) + ms[3]
        fp, cp = conv_ffn(hp, jnp.zeros((b, CONV_W - 1, D_FF), hp.dtype), w_ffn_up[i], ffn_conv_w[i], ffn_conv_b[i], w_ffn_down[i])
        fs, cs = conv_ffn(hs, state_ffn_conv[i], w_ffn_up[i], ffn_conv_w[i], ffn_conv_b[i], w_ffn_down[i])
        xp = xp + mp[5] * fp
        xs = xs + ms[5] * fs
        ffn_p.append(cp)
        ffn_s.append(cs)
    return (xp, xs, jnp.stack(mla_p), jnp.stack(mla_s), jnp.stack(sb_p), jnp.stack(sb_s),
            jnp.stack(hg_p), jnp.stack(hg_s), jnp.stack(ffn_p), jnp.stack(ffn_s))
```

```python
import functools

import jax
import jax.numpy as jnp
import numpy as np
from jax import lax
from jax.experimental import pallas as pl
from jax.experimental.pallas import tpu as pltpu

F32 = jnp.float32
BF16 = jnp.bfloat16

N_MIXERS = 3
MLA_HEADS = 8
QK_NOPE = 128
QK_ROPE = 64
V_HEAD = 128
Q_LORA = 256
KV_LORA = 256
QK_HEAD = QK_NOPE + QK_ROPE
MLA_SCALE = QK_HEAD ** -0.5
ROPE_BASE = 10000.0
SB_HEADS = 8
SB_HEAD = 128
SB_SCALE = SB_HEAD ** -0.5
HG_HEADS = 8
HG_DK = 128
HG_DV = 128
HG_CHUNK = 32
CONV_W = 3
PAGE_SIZE = 128
EPS = 1e-6
N_MOD = 6

LANES = 128
SUBLANES = 8
T_PAD = SUBLANES
NEG_BIG = -1e30
HEAD_SHIFT = 3
LANE_SHIFT = 7
VMEM_LIMIT = 48 * 1024 * 1024


def _cparams(sem):
    return pltpu.CompilerParams(dimension_semantics=sem, vmem_limit_bytes=VMEM_LIMIT)


def _dot(a, b):
    return jnp.dot(a, b, preferred_element_type=F32)


def _dot_nt(a, b):
    return lax.dot_general(a, b, (((1,), (1,)), ((), ())), preferred_element_type=F32)


def _dot_tn(a, b):
    return lax.dot_general(a, b, (((0,), (0,)), ((), ())), preferred_element_type=F32)


def _split_bf16(x, parts):
    out = []
    r = x
    for _ in range(parts):
        p = r.astype(BF16)
        out.append(p)
        r = r - p.astype(F32)
    return out


def _dot01(m01, x, parts=3):
    acc = None
    for p in _split_bf16(x, parts):
        d = _dot(m01, p)
        acc = d if acc is None else acc + d
    return acc


def _dot01_right(x, m01, parts=2):
    acc = None
    for p in _split_bf16(x, parts):
        d = _dot(p, m01)
        acc = d if acc is None else acc + d
    return acc


def _rms(x, g):
    ms = jnp.mean(x * x, axis=-1, keepdims=True)
    return x * lax.rsqrt(ms + EPS) * g


def _norm_mod(x, g, shift, scale):
    return _rms(x, g) * (1.0 + scale) + shift


def _silu(x):
    return x * jax.nn.sigmoid(x)


def _mod_kernel(c_ref, w_ref, b_ref, o_ref):
    s = _silu(c_ref[...])
    o_ref[...] = _dot(s.astype(BF16), w_ref[...].astype(BF16)) + b_ref[...]


def ada_mod(c_all, w_mod, b_mod):
    depth, d, n = w_mod.shape
    nc = c_all.shape[0]
    tn = n // 4
    return pl.pallas_call(
        _mod_kernel,
        out_shape=jax.ShapeDtypeStruct((depth, nc, n), F32),
        grid=(depth, n // tn),
        in_specs=[
            pl.BlockSpec((nc, d), lambda l, j: (0, 0)),
            pl.BlockSpec((None, d, tn), lambda l, j: (l, 0, j)),
            pl.BlockSpec((None, 1, tn), lambda l, j: (l, 0, j)),
        ],
        out_specs=pl.BlockSpec((None, nc, tn), lambda l, j: (l, 0, j)),
        compiler_params=_cparams(("arbitrary", "arbitrary")),
        name="ada_mod",
    )(c_all, w_mod, b_mod.reshape(depth, 1, n))


def _mod_spec(mod, tm):
    r = mod.shape[1]
    if r == 1:
        return pl.BlockSpec((None, 1, mod.shape[2]), lambda b, i, *_: (b, 0, 0))
    return pl.BlockSpec((None, tm, mod.shape[2]), lambda b, i, *_: (b, i, 0))


def _nml_kernel(x_ref, g_ref, sh_ref, sc_ref, *rest, n_w):
    h = _norm_mod(x_ref[...], g_ref[...], sh_ref[...], sc_ref[...]).astype(BF16)
    for w_ref, o_ref in zip(rest[:n_w], rest[n_w:]):
        o_ref[...] = _dot(h, w_ref[...]).astype(o_ref.dtype)


def norm_mod_linear(x, g, shift, scale, ws, tm):
    nb, s, d = x.shape
    tm = min(tm, s)
    in_specs = [
        pl.BlockSpec((None, tm, d), lambda b, i: (b, i, 0)),
        pl.BlockSpec((1, d), lambda b, i: (0, 0)),
        _mod_spec(shift, tm),
        _mod_spec(scale, tm),
    ]
    out_shape, out_specs = [], []
    for w in ws:
        in_specs.append(pl.BlockSpec(w.shape, lambda b, i: (0, 0)))
        out_shape.append(jax.ShapeDtypeStruct((nb, s, w.shape[1]), F32))
        out_specs.append(pl.BlockSpec((None, tm, w.shape[1]), lambda b, i: (b, i, 0)))
    return pl.pallas_call(
        functools.partial(_nml_kernel, n_w=len(ws)),
        out_shape=out_shape,
        grid=(nb, s // tm),
        in_specs=in_specs,
        out_specs=out_specs,
        compiler_params=_cparams(("arbitrary", "arbitrary")),
        name="norm_mod_linear",
    )(x, g.reshape(1, d), shift, scale, *ws)


def _proj_res_kernel(o_ref, w_ref, x_ref, gate_ref, out_ref):
    y = _dot(o_ref[...].astype(BF16), w_ref[...])
    out_ref[...] = x_ref[...] + gate_ref[...] * y


def proj_residual(o, w, x, gate, tm):
    nb, s, d = x.shape
    din = o.shape[2]
    tm = min(tm, s)
    return pl.pallas_call(
        _proj_res_kernel,
        out_shape=jax.ShapeDtypeStruct((nb, s, d), F32),
        grid=(nb, s // tm),
        in_specs=[
            pl.BlockSpec((None, tm, din), lambda b, i: (b, i, 0)),
            pl.BlockSpec((din, d), lambda b, i: (0, 0)),
            pl.BlockSpec((None, tm, d), lambda b, i: (b, i, 0)),
            _mod_spec(gate, tm),
        ],
        out_specs=pl.BlockSpec((None, tm, d), lambda b, i: (b, i, 0)),
        compiler_params=_cparams(("arbitrary", "arbitrary")),
        name="proj_residual",
    )(o, w, x, gate)


def _mla_proj_kernel(x_ref, g_ref, sh_ref, sc_ref, cos_ref, sin_ref, wa_ref, gqa_ref, gkva_ref,
                     wqm_ref, wqs_ref, wkv_ref, gq_ref, gk_ref,
                     rows_ref, q_ref, k_ref, v_ref):
    h = _norm_mod(x_ref[...], g_ref[...], sh_ref[...], sc_ref[...]).astype(BF16)
    a = _dot(h, wa_ref[...])
    cos = cos_ref[...]
    sin = sin_ref[...]
    qn = _rms(a[:, :Q_LORA], gqa_ref[...]).astype(BF16)
    cn = _rms(a[:, Q_LORA:Q_LORA + KV_LORA], gkva_ref[...])
    o = Q_LORA + KV_LORA
    kpe = a[:, o:o + LANES] * cos + a[:, o + LANES:o + 2 * LANES] * sin
    rows_ref[...] = jnp.concatenate([cn, kpe[:, :QK_ROPE]], axis=-1)
    qm = _dot(qn, wqm_ref[...])
    qs = _dot(qn, wqs_ref[...])
    kv = _dot(cn.astype(BF16), wkv_ref[...])
    gq = gq_ref[...]
    gk = gk_ref[...]
    kpe_ss = jnp.sum(kpe * kpe, axis=-1, keepdims=True)
    inv = 1.0 / QK_HEAD
    for hd in range(MLA_HEADS):
        c0 = 2 * LANES * hd
        nope = qm[:, c0:c0 + LANES]
        pe = qm[:, c0 + LANES:c0 + 2 * LANES] * cos + qs[:, LANES * hd:LANES * (hd + 1)] * sin
        ss = jnp.sum(nope * nope, axis=-1, keepdims=True) + jnp.sum(pe * pe, axis=-1, keepdims=True)
        r = lax.rsqrt(ss * inv + EPS) * MLA_SCALE
        q_ref[:, c0:c0 + LANES] = (nope * r * gq[:, :LANES]).astype(q_ref.dtype)
        q_ref[:, c0 + LANES:c0 + 2 * LANES] = (pe * r * gq[:, LANES:]).astype(q_ref.dtype)
        kn = kv[:, c0:c0 + LANES]
        rk = lax.rsqrt((jnp.sum(kn * kn, axis=-1, keepdims=True) + kpe_ss) * inv + EPS)
        k_ref[:, c0:c0 + LANES] = (kn * rk * gk[:, :LANES]).astype(k_ref.dtype)
        k_ref[:, c0 + LANES:c0 + 2 * LANES] = (kpe * rk * gk[:, LANES:]).astype(k_ref.dtype)
        v_ref[:, LANES * hd:LANES * (hd + 1)] = kv[:, c0 + LANES:c0 + 2 * LANES].astype(v_ref.dtype)


def mla_proj(x, g, shift, scale, cos, sin, wp, q_dtype, tm):
    nb, s, d = x.shape
    tm = min(tm, s)
    hw = 2 * LANES * MLA_HEADS
    full = lambda arr: pl.BlockSpec(arr.shape, lambda b, i: (0, 0))
    return pl.pallas_call(
        _mla_proj_kernel,
        out_shape=[
            jax.ShapeDtypeStruct((nb, s, KV_LORA + QK_ROPE), F32),
            jax.ShapeDtypeStruct((nb, s, hw), q_dtype),
            jax.ShapeDtypeStruct((nb, s, hw), BF16),
            jax.ShapeDtypeStruct((nb, s, LANES * MLA_HEADS), BF16),
        ],
        grid=(nb, s // tm),
        in_specs=[
            pl.BlockSpec((None, tm, d), lambda b, i: (b, i, 0)),
            pl.BlockSpec((1, d), lambda b, i: (0, 0)),
            _mod_spec(shift, tm),
            _mod_spec(scale, tm),
            pl.BlockSpec((tm, LANES), lambda b, i: (i, 0)),
            pl.BlockSpec((tm, LANES), lambda b, i: (i, 0)),
            full(wp["wa"]), full(wp["gqa"]), full(wp["gkva"]), full(wp["wqm"]), full(wp["wqs"]),
            full(wp["wkv"]), full(wp["gq"]), full(wp["gk"]),
        ],
        out_specs=[
            pl.BlockSpec((None, tm, KV_LORA + QK_ROPE), lambda b, i: (b, i, 0)),
            pl.BlockSpec((None, tm, hw), lambda b, i: (b, i, 0)),
            pl.BlockSpec((None, tm, hw), lambda b, i: (b, i, 0)),
            pl.BlockSpec((None, tm, LANES * MLA_HEADS), lambda b, i: (b, i, 0)),
        ],
        compiler_params=_cparams(("arbitrary", "arbitrary")),
        name="mla_proj",
    )(x, g.reshape(1, d), shift, scale, cos, sin, wp["wa"], wp["gqa"], wp["gkva"], wp["wqm"], wp["wqs"],
      wp["wkv"], wp["gq"], wp["gk"])


def _mla_weights(w_a, g_qa, g_kva, w_uq, w_ukv, g_q, g_k):
    d = w_a.shape[0]
    half = QK_ROPE // 2
    zpad = jnp.zeros((d, LANES - QK_ROPE), F32)
    w_pe = w_a[:, Q_LORA + KV_LORA:]
    w_pe_sw = jnp.concatenate([w_pe[:, half:], w_pe[:, :half]], axis=1)
    wa = jnp.concatenate([w_a[:, :Q_LORA + KV_LORA], w_pe, zpad, w_pe_sw, zpad], axis=1).astype(BF16)
    wq = w_uq.reshape(Q_LORA, MLA_HEADS, QK_HEAD)
    wq_pe = wq[:, :, QK_NOPE:]
    z = jnp.zeros((Q_LORA, MLA_HEADS, LANES - QK_ROPE), F32)
    wqm = jnp.concatenate([wq, z], axis=2).reshape(Q_LORA, MLA_HEADS * 2 * LANES).astype(BF16)
    wq_sw = jnp.concatenate([wq_pe[:, :, half:], wq_pe[:, :, :half], z], axis=2)
    wqs = wq_sw.reshape(Q_LORA, MLA_HEADS * LANES).astype(BF16)
    gpad = jnp.zeros((2 * LANES - QK_HEAD,), F32)
    wkv3 = w_ukv.reshape(KV_LORA, MLA_HEADS, QK_NOPE + V_HEAD)
    return {
        "wa": wa,
        "gqa": g_qa.reshape(1, Q_LORA),
        "gkva": g_kva.reshape(1, KV_LORA),
        "wqm": wqm,
        "wqs": wqs,
        "wkv": w_ukv.astype(BF16),
        "gq": jnp.concatenate([g_q, gpad]).reshape(1, 2 * LANES),
        "gk": jnp.concatenate([g_k, gpad]).reshape(1, 2 * LANES),
        "wk_t": wkv3[:, :, :QK_NOPE].reshape(KV_LORA, MLA_HEADS * QK_NOPE).T.astype(BF16),
        "wk": wkv3[:, :, :QK_NOPE].reshape(KV_LORA, MLA_HEADS * QK_NOPE).astype(BF16),
        "wv": wkv3[:, :, QK_NOPE:].reshape(KV_LORA, MLA_HEADS * V_HEAD).astype(BF16),
    }


def _rope_tables(pos):
    half = QK_ROPE // 2
    inv = ROPE_BASE ** (-jnp.arange(half, dtype=F32) / half)
    ang = pos.astype(F32)[:, None] * inv[None, :]
    cos, sin = jnp.cos(ang), jnp.sin(ang)
    z = jnp.zeros((pos.shape[0], LANES - QK_ROPE), F32)
    return jnp.concatenate([cos, cos, z], axis=1), jnp.concatenate([-sin, sin, z], axis=1)


def _pairs(n_q, tq, tk, newest_first):
    qi, ki = [], []
    for a in range(n_q):
        last = ((a + 1) * tq - 1) // tk
        ks = range(last, -1, -1) if newest_first else range(last + 1)
        for b in ks:
            qi.append(a)
            ki.append(b)
    return jnp.asarray(qi, jnp.int32), jnp.asarray(ki, jnp.int32)


def _flash_kernel(qi_ref, ki_ref, q_ref, k_ref, v_ref, o_ref, m_sc, l_sc, acc_sc, *, tq, tk):
    p = pl.program_id(2)
    qi = qi_ref[p]
    ki = ki_ref[p]

    @pl.when(ki == 0)
    def _():
        m_sc[...] = jnp.full(m_sc.shape, NEG_BIG, F32)
        l_sc[...] = jnp.zeros(l_sc.shape, F32)
        acc_sc[...] = jnp.zeros(acc_sc.shape, F32)

    s = _dot_nt(q_ref[...], k_ref[...])
    row = qi * tq + lax.broadcasted_iota(jnp.int32, (tq, tk), 0)
    col = ki * tk + lax.broadcasted_iota(jnp.int32, (tq, tk), 1)
    s = jnp.where(row >= col, s, NEG_BIG)
    m_prev = m_sc[...]
    m_new = jnp.maximum(m_prev, jnp.max(s, axis=-1, keepdims=True))
    alpha = jnp.exp(m_prev - m_new)
    pr = jnp.exp(s - m_new)
    l_sc[...] = alpha * l_sc[...] + jnp.sum(pr, axis=-1, keepdims=True)
    acc_sc[...] = alpha * acc_sc[...] + _dot(pr.astype(BF16), v_ref[...])
    m_sc[...] = m_new

    @pl.when(ki == ((qi + 1) * tq - 1) // tk)
    def _():
        o_ref[...] = (acc_sc[...] / l_sc[...]).astype(o_ref.dtype)


def flash_mla(q, k, v, tq, tk):
    nb, s, _ = q.shape
    tq, tk = min(tq, s), min(tk, s)
    qi, ki = _pairs(s // tq, tq, tk, newest_first=False)
    grid_spec = pltpu.PrefetchScalarGridSpec(
        num_scalar_prefetch=2,
        grid=(nb, MLA_HEADS, qi.shape[0]),
        in_specs=[
            pl.BlockSpec((None, tq, 2 * LANES), lambda b, h, p, qi, ki: (b, qi[p], h)),
            pl.BlockSpec((None, tk, 2 * LANES), lambda b, h, p, qi, ki: (b, ki[p], h)),
            pl.BlockSpec((None, tk, LANES), lambda b, h, p, qi, ki: (b, ki[p], h)),
        ],
        out_specs=pl.BlockSpec((None, tq, LANES), lambda b, h, p, qi, ki: (b, qi[p], h)),
        scratch_shapes=[pltpu.VMEM((tq, 1), F32), pltpu.VMEM((tq, 1), F32), pltpu.VMEM((tq, LANES), F32)],
    )
    return pl.pallas_call(
        functools.partial(_flash_kernel, tq=tq, tk=tk),
        out_shape=jax.ShapeDtypeStruct((nb, s, MLA_HEADS * LANES), BF16),
        grid_spec=grid_spec,
        compiler_params=_cparams(("arbitrary", "arbitrary", "arbitrary")),
        name="flash_mla",
    )(qi, ki, q, k, v)


def _sb_terms(z):
    tail = jnp.log1p(jnp.exp(-jnp.abs(z)))
    return -(jnp.maximum(z, 0.0) + tail), jnp.minimum(z, 0.0) - tail


def _sb_kernel(qi_ref, ki_ref, q_ref, k_ref, v_ref, u_ref, o_ref, keep_sc, acc_sc, *, tq, tk):
    p = pl.program_id(2)
    qi = qi_ref[p]
    ki = ki_ref[p]

    @pl.when(ki == ((qi + 1) * tq - 1) // tk)
    def _():
        keep_sc[...] = jnp.zeros(keep_sc.shape, F32)
        acc_sc[...] = jnp.zeros(acc_sc.shape, F32)

    q = (q_ref[...] * SB_SCALE).astype(BF16)
    z = _dot_nt(q, k_ref[...].astype(BF16))
    log_keep, log_break = _sb_terms(z)
    row = qi * tq + lax.broadcasted_iota(jnp.int32, (tq, tk), 0)
    col = ki * tk + lax.broadcasted_iota(jnp.int32, (tq, tk), 1)
    mask = row > col
    log_keep = jnp.where(mask, log_keep, 0.0)
    later = _dot01_right(log_keep, u_ref[...]) + keep_sc[...]
    a = jnp.where(mask, jnp.exp(log_break + later), 0.0)
    acc_sc[...] += _dot(a.astype(BF16), v_ref[...].astype(BF16))
    keep_sc[...] += jnp.sum(log_keep, axis=-1, keepdims=True)

    @pl.when(ki == 0)
    def _():
        o_ref[...] = acc_sc[...].astype(o_ref.dtype)


def _later_matrix(n):
    return (lax.broadcasted_iota(jnp.int32, (n, n), 0) > lax.broadcasted_iota(jnp.int32, (n, n), 1)).astype(BF16)


def sb_attention(q, kv, tq, tk):
    nb, s, _ = q.shape
    tq, tk = min(tq, s), min(tk, s)
    qi, ki = _pairs(s // tq, tq, tk, newest_first=True)
    grid_spec = pltpu.PrefetchScalarGridSpec(
        num_scalar_prefetch=2,
        grid=(nb, SB_HEADS, qi.shape[0]),
        in_specs=[
            pl.BlockSpec((None, tq, SB_HEAD), lambda b, h, p, qi, ki: (b, qi[p], h)),
            pl.BlockSpec((None, tk, SB_HEAD), lambda b, h, p, qi, ki: (b, ki[p], h)),
            pl.BlockSpec((None, tk, SB_HEAD), lambda b, h, p, qi, ki: (b, ki[p], SB_HEADS + h)),
            pl.BlockSpec((tk, tk), lambda b, h, p, qi, ki: (0, 0)),
        ],
        out_specs=pl.BlockSpec((None, tq, SB_HEAD), lambda b, h, p, qi, ki: (b, qi[p], h)),
        scratch_shapes=[pltpu.VMEM((tq, 1), F32), pltpu.VMEM((tq, SB_HEAD), F32)],
    )
    return pl.pallas_call(
        functools.partial(_sb_kernel, tq=tq, tk=tk),
        out_shape=jax.ShapeDtypeStruct((nb, s, SB_HEADS * SB_HEAD), BF16),
        grid_spec=grid_spec,
        compiler_params=_cparams(("arbitrary", "arbitrary", "arbitrary")),
        name="sb_attention",
    )(qi, ki, q, kv, kv, _later_matrix(tk))


def _hgrn_kernel(x_ref, lbl_ref, st0_ref, go_ref, mc_ref, ml_ref, o_ref, stf_ref,
                 st_sc, qd_sc, kd_sc, kl_sc, el_sc, *, layer, chunk, t_valid, ts):
    s = pl.program_id(1)
    hk = HG_HEADS * HG_DK
    cdt = BF16 if chunk % 16 == 0 else F32

    @pl.when(s == 0)
    def _():
        for hd in range(HG_HEADS):
            st_sc[hd] = st0_ref[hd].T

    logits = lbl_ref[...]
    e = jnp.exp(logits - jnp.max(logits, axis=0, keepdims=True))
    sm = e / jnp.sum(e, axis=0, keepdims=True)
    lb = jnp.sum(sm[1:layer + 1], axis=0, keepdims=True)
    q = x_ref[:, :hk]
    forget = lb + (1.0 - lb) * jax.nn.sigmoid(x_ref[:, hk:2 * hk])
    k = 1.0 - forget
    log_f = jnp.log(forget)
    if t_valid < ts:
        valid = lax.broadcasted_iota(jnp.int32, (ts, hk), 0) < t_valid
        log_f = jnp.where(valid, log_f, 0.0)
        k = jnp.where(valid, k, 0.0)
    if cdt == BF16:
        cum = _dot01(mc_ref[...], log_f)
        last = _dot01(ml_ref[...], log_f)
    else:
        cum = jnp.dot(mc_ref[...], log_f, precision=lax.Precision.HIGHEST, preferred_element_type=F32)
        last = jnp.dot(ml_ref[...], log_f, precision=lax.Precision.HIGHEST, preferred_element_type=F32)
    qd_sc[...] = q * jnp.exp(cum)
    kd_sc[...] = k * jnp.exp(-cum)
    kl_sc[...] = k * jnp.exp(last - cum)
    el_sc[...] = jnp.exp(last)
    go = go_ref[...]
    causal = (lax.broadcasted_iota(jnp.int32, (chunk, chunk), 0)
              >= lax.broadcasted_iota(jnp.int32, (chunk, chunk), 1))

    def chunk_body(c, carry):
        r0 = pl.multiple_of(c * chunk, chunk)
        rows = pl.ds(r0, chunk)
        for hd in range(HG_HEADS):
            cols = slice(HG_DK * hd, HG_DK * (hd + 1))
            qd = qd_sc[rows, cols].astype(cdt)
            kd = kd_sc[rows, cols].astype(cdt)
            kl = kl_sc[rows, cols].astype(cdt)
            v = x_ref[rows, 2 * hk + HG_DV * hd:2 * hk + HG_DV * (hd + 1)].astype(cdt)
            gate = x_ref[rows, 2 * hk + HG_HEADS * HG_DV + HG_DV * hd:2 * hk + HG_HEADS * HG_DV + HG_DV * (hd + 1)]
            el = el_sc[pl.ds(r0, 1), cols]
            st = st_sc[hd]
            att = jnp.where(causal, _dot_nt(qd, kd), 0.0)
            o = _dot_nt(qd, st.astype(cdt)) + _dot(att.astype(cdt), v)
            st_sc[hd] = st * el + _dot_tn(v, kl)
            on = _rms(o, go) * _silu(gate)
            o_ref[rows, HG_DV * hd:HG_DV * (hd + 1)] = on.astype(o_ref.dtype)
        return carry

    lax.fori_loop(0, ts // chunk, chunk_body, 0)

    @pl.when(s == pl.num_programs(1) - 1)
    def _():
        for hd in range(HG_HEADS):
            stf_ref[hd] = st_sc[hd].T


def _chunk_matrices(ts, chunk):
    r = lax.broadcasted_iota(jnp.int32, (ts, ts), 0)
    c = lax.broadcasted_iota(jnp.int32, (ts, ts), 1)
    same = (r // chunk) == (c // chunk)
    dtype = BF16 if chunk % 16 == 0 else F32
    return (same & (c <= r)).astype(dtype), same.astype(dtype)


def hgrn_scan(x, lb_logits, state0, g_o, layer, chunk, t_valid, ts, out_dtype):
    nb, s, _ = x.shape
    ts = min(ts, s)
    hk, hv = HG_HEADS * HG_DK, HG_HEADS * HG_DV
    mc, ml = _chunk_matrices(ts, chunk)
    return pl.pallas_call(
        functools.partial(_hgrn_kernel, layer=layer, chunk=chunk, t_valid=t_valid, ts=ts),
        out_shape=[
            jax.ShapeDtypeStruct((nb, s, hv), out_dtype),
            jax.ShapeDtypeStruct((nb, HG_HEADS, HG_DK, HG_DV), F32),
        ],
        grid=(nb, s // ts),
        in_specs=[
            pl.BlockSpec((None, ts, 2 * hk + 2 * hv), lambda b, i: (b, i, 0)),
            pl.BlockSpec(lb_logits.shape, lambda b, i: (0, 0)),
            pl.BlockSpec((None, HG_HEADS, HG_DK, HG_DV), lambda b, i: (b, 0, 0, 0)),
            pl.BlockSpec((1, HG_DV), lambda b, i: (0, 0)),
            pl.BlockSpec((ts, ts), lambda b, i: (0, 0)),
            pl.BlockSpec((ts, ts), lambda b, i: (0, 0)),
        ],
        out_specs=[
            pl.BlockSpec((None, ts, hv), lambda b, i: (b, i, 0)),
            pl.BlockSpec((None, HG_HEADS, HG_DK, HG_DV), lambda b, i: (b, 0, 0, 0)),
        ],
        scratch_shapes=[
            pltpu.VMEM((HG_HEADS, HG_DV, HG_DK), F32),
            pltpu.VMEM((ts, hk), F32),
            pltpu.VMEM((ts, hk), F32),
            pltpu.VMEM((ts, hk), F32),
            pltpu.VMEM((ts, hk), F32),
        ],
        compiler_params=_cparams(("arbitrary", "arbitrary")),
        name="hgrn_scan",
    )(x, lb_logits, state0, g_o.reshape(1, HG_DV), mc, ml)


def _ffn_kernel(*refs, tm, seg, stream):
    if stream:
        (x_ref, g_ref, sh_ref, sc_ref, gate_ref, wa_ref, wu_ref, cw_ref, cb_ref, wd_ref,
         y_ref, tail_ref, h_sc, acc_sc, carry_sc) = refs
    else:
        (x_ref, g_ref, sh_ref, sc_ref, gate_ref, wa_ref, wu_ref, cw_ref, cb_ref, wd_ref, past_ref,
         y_ref, tail_ref, h_sc, acc_sc) = refs
    i = pl.program_id(1)
    f = pl.program_id(2)

    @pl.when(f == 0)
    def _():
        h_sc[...] = _norm_mod(x_ref[...], g_ref[...], sh_ref[...], sc_ref[...]).astype(BF16)
        acc_sc[...] = jnp.zeros(acc_sc.shape, F32)

    h = h_sc[...]
    a = _dot(h, wa_ref[...])
    u = _dot(h, wu_ref[...])
    tf = a.shape[1]
    row = lax.broadcasted_iota(jnp.int32, (tm, tf), 0)
    a1 = pltpu.roll(a, 1, 0)
    a2 = pltpu.roll(a, 2, 0)
    if stream:
        @pl.when(i == 0)
        def _():
            carry_sc[f] = jnp.zeros(carry_sc.shape[1:], F32)

        c = carry_sc[f]
        a1 = jnp.where(row == 0, c[SUBLANES - 1:SUBLANES], a1)
        a2 = jnp.where(row == 0, c[SUBLANES - 2:SUBLANES - 1], jnp.where(row == 1, c[SUBLANES - 1:SUBLANES], a2))
        carry_sc[f] = a[tm - SUBLANES:, :]
        tail_ref[...] = a[tm - SUBLANES:, :]
    else:
        t = jnp.bitwise_and(row, seg - 1)
        a1 = jnp.where(t == 0, past_ref[1], a1)
        a2 = jnp.where(t < 2, past_ref[0], a2)
        tail_ref[...] = a
    cw = cw_ref[...]
    conv = cb_ref[...] + a2 * cw[0:1] + a1 * cw[1:2] + a * cw[2:3]
    mid = (_silu(conv) * u).astype(BF16)
    acc_sc[...] += _dot(mid, wd_ref[...])

    @pl.when(f == pl.num_programs(2) - 1)
    def _():
        y_ref[...] = x_ref[...] + gate_ref[...] * acc_sc[...]


def conv_ffn(x, g, shift, scale, gate, w_up, conv_w, conv_b, w_down, past, tm, tf):
    nb, s, d = x.shape
    dff = w_down.shape[0]
    tm = min(tm, s)
    nf = dff // tf
    stream = past is None
    in_specs = [
        pl.BlockSpec((None, tm, d), lambda b, i, f: (b, i, 0)),
        pl.BlockSpec((1, d), lambda b, i, f: (0, 0)),
        _mod_spec(shift, tm),
        _mod_spec(scale, tm),
        _mod_spec(gate, tm),
        pl.BlockSpec((d, tf), lambda b, i, f: (0, f)),
        pl.BlockSpec((d, tf), lambda b, i, f: (0, nf + f)),
        pl.BlockSpec((CONV_W, tf), lambda b, i, f: (0, f)),
        pl.BlockSpec((1, tf), lambda b, i, f: (0, f)),
        pl.BlockSpec((tf, d), lambda b, i, f: (f, 0)),
    ]
    args = [x, g.reshape(1, d), shift, scale, gate, w_up, w_up, conv_w, conv_b.reshape(1, dff), w_down]
    scratch = [pltpu.VMEM((tm, d), BF16), pltpu.VMEM((tm, d), F32)]
    if stream:
        tail_shape = jax.ShapeDtypeStruct((nb, s // tm, SUBLANES, dff), F32)
        tail_spec = pl.BlockSpec((None, None, SUBLANES, tf), lambda b, i, f: (b, i, 0, f))
        scratch.append(pltpu.VMEM((nf, SUBLANES, tf), F32))
    else:
        in_specs.append(pl.BlockSpec((CONV_W - 1, tm, tf), lambda b, i, f: (0, i, f)))
        args.append(past)
        tail_shape = jax.ShapeDtypeStruct((nb, s, dff), F32)
        tail_spec = pl.BlockSpec((None, tm, tf), lambda b, i, f: (b, i, f))
    return pl.pallas_call(
        functools.partial(_ffn_kernel, tm=tm, seg=T_PAD, stream=stream),
        out_shape=[jax.ShapeDtypeStruct((nb, s, d), F32), tail_shape],
        grid=(nb, s // tm, nf),
        in_specs=in_specs,
        out_specs=[pl.BlockSpec((None, tm, d), lambda b, i, f: (b, i, 0)), tail_spec],
        scratch_shapes=scratch,
        compiler_params=_cparams(("arbitrary", "arbitrary", "arbitrary")),
        name="conv_ffn",
    )(*args)


def _absorb_kernel(q_ref, wk_ref, gk_ref, qa_ref, qp_ref):
    gk = gk_ref[...]
    for hd in range(MLA_HEADS):
        c0 = 2 * LANES * hd
        qn = (q_ref[:, c0:c0 + LANES] * gk[:, :LANES]).astype(BF16)
        qa_ref[:, c0:c0 + 2 * LANES] = _dot_nt(qn, wk_ref[:, LANES * hd:LANES * (hd + 1)]).astype(qa_ref.dtype)
        qp_ref[:, LANES * hd:LANES * (hd + 1)] = (q_ref[:, c0 + LANES:c0 + 2 * LANES] * gk[:, LANES:]).astype(qp_ref.dtype)


def mla_absorb(q, wk, gk):
    m = q.shape[0]
    return pl.pallas_call(
        _absorb_kernel,
        out_shape=[jax.ShapeDtypeStruct((m, MLA_HEADS * KV_LORA), BF16),
                   jax.ShapeDtypeStruct((m, MLA_HEADS * LANES), BF16)],
        name="mla_absorb",
    )(q, wk, gk)


def _vup_kernel(o_ref, wv_ref, out_ref):
    for hd in range(MLA_HEADS):
        lat = o_ref[:, KV_LORA * hd:KV_LORA * (hd + 1)].astype(BF16)
        out_ref[:, V_HEAD * hd:V_HEAD * (hd + 1)] = _dot(lat, wv_ref[:, V_HEAD * hd:V_HEAD * (hd + 1)])


def mla_vup(o_lat, wv):
    m = o_lat.shape[0]
    return pl.pallas_call(
        _vup_kernel,
        out_shape=jax.ShapeDtypeStruct((m, MLA_HEADS * V_HEAD), F32),
        name="mla_vup",
    )(o_lat, wv)


def _mla_dec_kernel(pt_ref, new_ref, *refs, n_slots):
    page_refs = refs[:n_slots]
    qa_ref, qp_ref, wkt_ref, o_ref, m_sc, l_sc, acc_sc = refs[n_slots:]
    s = pl.program_id(1)
    n_rows = T_PAD * MLA_HEADS
    eye = (lax.broadcasted_iota(jnp.int32, (QK_ROPE, QK_ROPE), 0)
           == lax.broadcasted_iota(jnp.int32, (QK_ROPE, QK_ROPE), 1)).astype(BF16)

    def process(tile, causal):
        cb = tile[:, :KV_LORA].astype(BF16)
        kpe = tile[:, KV_LORA:]
        kt = _dot_nt(wkt_ref[...], cb)
        hi, lo = _split_bf16(kpe, 2)
        kpt = _dot_nt(eye, hi) + _dot_nt(eye, lo)
        ss_pe = jnp.sum(kpt * kpt, axis=0, keepdims=True)
        ss = jnp.concatenate(
            [jnp.sum(kt[QK_NOPE * hd:QK_NOPE * (hd + 1)] ** 2, axis=0, keepdims=True) for hd in range(MLA_HEADS)],
            axis=0)
        r = lax.rsqrt((ss + ss_pe) * (1.0 / QK_HEAD) + EPS)
        sc = _dot_nt(qa_ref[...], cb) + _dot(qp_ref[...], kpt.astype(BF16))
        sc = sc * jnp.concatenate([r] * T_PAD, axis=0)
        if causal:
            tok = lax.shift_right_logical(lax.broadcasted_iota(jnp.int32, sc.shape, 0), HEAD_SHIFT)
            key = lax.broadcasted_iota(jnp.int32, sc.shape, 1)
            sc = jnp.where(key <= tok, sc, NEG_BIG)
        m_prev = m_sc[...]
        m_new = jnp.maximum(m_prev, jnp.max(sc, axis=-1, keepdims=True))
        alpha = jnp.exp(m_prev - m_new)
        pr = jnp.exp(sc - m_new)
        l_sc[...] = alpha * l_sc[...] + jnp.sum(pr, axis=-1, keepdims=True)
        acc_sc[...] = alpha * acc_sc[...] + _dot(pr.astype(BF16), cb)
        m_sc[...] = m_new

    @pl.when(s == 0)
    def _():
        m_sc[...] = jnp.full((n_rows, 1), NEG_BIG, F32)
        l_sc[...] = jnp.zeros((n_rows, 1), F32)
        acc_sc[...] = jnp.zeros((n_rows, KV_LORA), F32)
        process(new_ref[...], True)

    @pl.when(s > 0)
    def _():
        for ref in page_refs:
            process(ref[...], False)

    @pl.when(s == pl.num_programs(1) - 1)
    def _():
        o_ref[...] = acc_sc[...] / l_sc[...]


def mla_decode(page_table, new_page, cache, layer, qa, qp, wk_t, n_slots):
    db, n_pages = page_table.shape
    n_rows = T_PAD * MLA_HEADS
    row_w = cache.shape[-1]
    steps = n_pages // n_slots

    def page_spec(slot):
        def imap(b, s, pt):
            idx = jnp.maximum(s - 1, 0) * n_slots + slot
            return (layer, pt[b * n_pages + idx], 0, 0)
        return pl.BlockSpec((None, None, PAGE_SIZE, row_w), imap)

    grid_spec = pltpu.PrefetchScalarGridSpec(
        num_scalar_prefetch=1,
        grid=(db, steps + 1),
        in_specs=[pl.BlockSpec((None, PAGE_SIZE, row_w), lambda b, s, pt: (b, 0, 0))]
        + [page_spec(i) for i in range(n_slots)]
        + [
            pl.BlockSpec((None, n_rows, KV_LORA), lambda b, s, pt: (b, 0, 0)),
            pl.BlockSpec((None, n_rows, QK_ROPE), lambda b, s, pt: (b, 0, 0)),
            pl.BlockSpec(wk_t.shape, lambda b, s, pt: (0, 0)),
        ],
        out_specs=pl.BlockSpec((None, n_rows, KV_LORA), lambda b, s, pt: (b, 0, 0)),
        scratch_shapes=[pltpu.VMEM((n_rows, 1), F32), pltpu.VMEM((n_rows, 1), F32),
                        pltpu.VMEM((n_rows, KV_LORA), F32)],
    )
    return pl.pallas_call(
        functools.partial(_mla_dec_kernel, n_slots=n_slots),
        out_shape=jax.ShapeDtypeStruct((db, n_rows, KV_LORA), F32),
        grid_spec=grid_spec,
        compiler_params=_cparams(("arbitrary", "arbitrary")),
        name="mla_decode",
    )(page_table.reshape(-1), new_page, *([cache] * n_slots), qa, qp, wk_t)


def _sb_dec_kernel(pt_ref, new_ref, *refs, n_slots):
    page_refs = refs[:n_slots]
    q_ref, u_ref, o_ref, qbd_sc, keep_sc, acc_sc = refs[n_slots:]
    s = pl.program_id(1)
    n_rows = T_PAD * SB_HEADS
    hw = SB_HEADS * SB_HEAD
    head_of_lane = lax.shift_right_logical(lax.broadcasted_iota(jnp.int32, (SB_HEADS, hw), 1), LANE_SHIFT)
    own_head = head_of_lane == lax.broadcasted_iota(jnp.int32, (SB_HEADS, hw), 0)

    def process(tile, causal):
        kb = tile[:, :hw].astype(BF16)
        vb = tile[:, hw:].astype(BF16)
        z = _dot_nt(qbd_sc[...], kb)
        log_keep, log_break = _sb_terms(z)
        if causal:
            tok = lax.shift_right_logical(lax.broadcasted_iota(jnp.int32, z.shape, 0), HEAD_SHIFT)
            key = lax.broadcasted_iota(jnp.int32, z.shape, 1)
            mask = key < tok
            log_keep = jnp.where(mask, log_keep, 0.0)
        later = _dot01_right(log_keep, u_ref[...]) + keep_sc[...]
        a = jnp.exp(log_break + later)
        if causal:
            a = jnp.where(mask, a, 0.0)
        acc_sc[...] += _dot(a.astype(BF16), vb)
        keep_sc[...] += jnp.sum(log_keep, axis=-1, keepdims=True)

    @pl.when(s == 0)
    def _():
        for t in range(T_PAD):
            qt = jnp.broadcast_to(q_ref[t:t + 1, :] * SB_SCALE, (SB_HEADS, hw))
            qbd_sc[SB_HEADS * t:SB_HEADS * (t + 1), :] = jnp.where(own_head, qt, 0.0).astype(BF16)
        keep_sc[...] = jnp.zeros((n_rows, 1), F32)
        acc_sc[...] = jnp.zeros((n_rows, hw), F32)
        process(new_ref[...], True)

    @pl.when(s > 0)
    def _():
        for ref in page_refs:
            process(ref[...], False)

    @pl.when(s == pl.num_programs(1) - 1)
    def _():
        for t in range(T_PAD):
            blk = jnp.where(own_head, acc_sc[SB_HEADS * t:SB_HEADS * (t + 1), :], 0.0)
            o_ref[t:t + 1, :] = jnp.sum(blk, axis=0, keepdims=True)


def sb_decode(page_table, new_page, cache, layer, q, n_slots):
    db, n_pages = page_table.shape
    n_rows = T_PAD * SB_HEADS
    hw = SB_HEADS * SB_HEAD
    steps = n_pages // n_slots

    def page_spec(slot):
        def imap(b, s, pt):
            idx = n_pages - 1 - (jnp.maximum(s - 1, 0) * n_slots + slot)
            return (layer, pt[b * n_pages + idx], 0, 0)
        return pl.BlockSpec((None, None, PAGE_SIZE, 2 * hw), imap)

    grid_spec = pltpu.PrefetchScalarGridSpec(
        num_scalar_prefetch=1,
        grid=(db, steps + 1),
        in_specs=[pl.BlockSpec((None, PAGE_SIZE, 2 * hw), lambda b, s, pt: (b, 0, 0))]
        + [page_spec(i) for i in range(n_slots)]
        + [
            pl.BlockSpec((None, T_PAD, hw), lambda b, s, pt: (b, 0, 0)),
            pl.BlockSpec((PAGE_SIZE, PAGE_SIZE), lambda b, s, pt: (0, 0)),
        ],
        out_specs=pl.BlockSpec((None, T_PAD, hw), lambda b, s, pt: (b, 0, 0)),
        scratch_shapes=[pltpu.VMEM((n_rows, hw), BF16), pltpu.VMEM((n_rows, 1), F32),
                        pltpu.VMEM((n_rows, hw), F32)],
    )
    return pl.pallas_call(
        functools.partial(_sb_dec_kernel, n_slots=n_slots),
        out_shape=jax.ShapeDtypeStruct((db, T_PAD, hw), F32),
        grid_spec=grid_spec,
        compiler_params=_cparams(("arbitrary", "arbitrary")),
        name="sb_decode",
    )(page_table.reshape(-1), new_page, *([cache] * n_slots), q, _later_matrix(PAGE_SIZE))


TM_LINEAR = 512
TM_FFN = 1024
TF_FFN = 256
TQ_MLA, TK_MLA = 512, 512
TQ_SB, TK_SB = 512, 256
TS_HGRN = 256
MLA_PAGES_PER_STEP = 4
SB_PAGES_PER_STEP = 4


def _pad_tokens(x, t_pad):
    return jnp.pad(x, ((0, 0), (0, t_pad - x.shape[1])) + ((0, 0),) * (x.ndim - 2))


def kernel(x_prompt, x_sample, c_prompt, c_sample, page_table, cache_mla, cache_sb_kv, state_hgrn,
           state_ffn_conv, w_mod, b_mod, norm_g, w_mla_a, g_mla_qa, g_mla_kva, w_mla_uq, w_mla_ukv,
           g_mla_q, g_mla_k, w_mla_o, w_sb_qkv, w_sb_o, w_hg_in, hg_lb_logits, g_hg_o, w_hg_o,
           w_ffn_up, ffn_conv_w, ffn_conv_b, w_ffn_down):
    b, s_len, d = x_prompt.shape
    db, t, _ = x_sample.shape
    depth = w_mod.shape[0]
    n_pages = page_table.shape[1]
    past = n_pages * PAGE_SIZE
    dff = w_ffn_down.shape[1]
    assert t <= T_PAD and t >= CONV_W - 1

    mod = ada_mod(jnp.concatenate([c_prompt, c_sample], axis=0), w_mod, b_mod)
    mod = mod.reshape(depth, b + db, N_MOD, d)
    cos_p, sin_p = _rope_tables(jnp.arange(s_len))
    cos_s, sin_s = _rope_tables(jnp.tile(past + jnp.arange(T_PAD), db))
    sb_cache = cache_sb_kv.reshape(cache_sb_kv.shape[:3] + (-1,))

    xp = x_prompt
    xs = _pad_tokens(x_sample, T_PAD).reshape(1, db * T_PAD, d)
    mla_p, mla_s, sb_p, sb_s, hg_p, hg_s, ffn_p, ffn_s = [], [], [], [], [], [], [], []
    for i in range(depth):
        mp = [mod[i, :b, j].reshape(b, 1, d) for j in range(N_MOD)]
        ms = [jnp.repeat(mod[i, b:, j], T_PAD, axis=0).reshape(1, db * T_PAD, d) for j in range(N_MOD)]
        kind, j = i % N_MIXERS, i // N_MIXERS
        if kind == 0:
            wp = _mla_weights(w_mla_a[j], g_mla_qa[j], g_mla_kva[j], w_mla_uq[j], w_mla_ukv[j],
                              g_mla_q[j], g_mla_k[j])
            rows_p, qp, kp, vp = mla_proj(xp, norm_g[i, 0], mp[0], mp[1], cos_p, sin_p, wp, BF16, TM_LINEAR)
            op = flash_mla(qp, kp, vp, TQ_MLA, TK_MLA)
            rows_s, qs, _, _ = mla_proj(xs, norm_g[i, 0], ms[0], ms[1], cos_s, sin_s, wp, F32, TM_LINEAR)
            qa, qpe = mla_absorb(qs[0], wp["wk"], wp["gk"])
            qa = qa.reshape(db, T_PAD * MLA_HEADS, KV_LORA)
            qpe = qpe.reshape(db, T_PAD * MLA_HEADS, LANES)[:, :, :QK_ROPE]
            rows_s = rows_s.reshape(db, T_PAD, -1)
            new_page = _pad_tokens(rows_s, PAGE_SIZE)
            o_lat = mla_decode(page_table, new_page, cache_mla, j, qa, qpe, wp["wk_t"], MLA_PAGES_PER_STEP)
            os_ = mla_vup(o_lat.reshape(db * T_PAD, MLA_HEADS * KV_LORA), wp["wv"])
            os_ = os_.reshape(1, db * T_PAD, -1)
            w_o = w_mla_o[j]
            mla_p.append(rows_p)
            mla_s.append(rows_s[:, :t])
        elif kind == 1:
            wq = w_sb_qkv[j].astype(BF16)
            hw = SB_HEADS * SB_HEAD
            q_p, kv_p = norm_mod_linear(xp, norm_g[i, 0], mp[0], mp[1], [wq[:, :hw], wq[:, hw:]], TM_LINEAR)
            op = sb_attention(q_p, kv_p, TQ_SB, TK_SB)
            q_s, kv_s = norm_mod_linear(xs, norm_g[i, 0], ms[0], ms[1], [wq[:, :hw], wq[:, hw:]], TM_LINEAR)
            kv_s = kv_s.reshape(db, T_PAD, 2 * hw)
            new_page = _pad_tokens(kv_s, PAGE_SIZE)
            os_ = sb_decode(page_table, new_page, sb_cache, j, q_s.reshape(db, T_PAD, hw), SB_PAGES_PER_STEP)
            os_ = os_.reshape(1, db * T_PAD, hw)
            w_o = w_sb_o[j]
            sb_p.append(kv_p.reshape(b, s_len, 2, SB_HEADS, SB_HEAD))
            sb_s.append(kv_s[:, :t].reshape(db, t, 2, SB_HEADS, SB_HEAD))
        else:
            w_in = w_hg_in[j].astype(BF16)
            (zp,) = norm_mod_linear(xp, norm_g[i, 0], mp[0], mp[1], [w_in], TM_LINEAR)
            zero_state = jnp.zeros((b, HG_HEADS, HG_DK, HG_DV), F32)
            op, st_p = hgrn_scan(zp, hg_lb_logits, zero_state, g_hg_o[j], i, HG_CHUNK, s_len, TS_HGRN, BF16)
            (zs,) = norm_mod_linear(xs, norm_g[i, 0], ms[0], ms[1], [w_in], TM_LINEAR)
            os_, st_s = hgrn_scan(zs.reshape(db, T_PAD, -1), hg_lb_logits, state_hgrn[j], g_hg_o[j], i,
                                  T_PAD, t, T_PAD, F32)
            os_ = os_.reshape(1, db * T_PAD, -1)
            w_o = w_hg_o[j]
            hg_p.append(st_p)
            hg_s.append(st_s)
        w_o = w_o.astype(BF16)
        xp = proj_residual(op, w_o, xp, mp[2], TM_LINEAR)
        xs = proj_residual(os_, w_o, xs, ms[2], TM_LINEAR)

        w_up = w_ffn_up[i].astype(BF16)
        w_down = w_ffn_down[i].astype(BF16)
        xp, tail_p = conv_ffn(xp, norm_g[i, 1], mp[3], mp[4], mp[5], w_up, ffn_conv_w[i], ffn_conv_b[i],
                              w_down, None, TM_FFN, TF_FFN)
        st = state_ffn_conv[i]
        z = jnp.zeros((db, T_PAD - 2, dff), F32)
        past2 = jnp.concatenate([st, z], axis=1)
        past1 = jnp.concatenate([st[:, 1:], z, z[:, :1]], axis=1)
        past_rows = jnp.stack([past2, past1]).reshape(CONV_W - 1, db * T_PAD, dff)
        xs, a_s = conv_ffn(xs, norm_g[i, 1], ms[3], ms[4], ms[5], w_up, ffn_conv_w[i], ffn_conv_b[i],
                           w_down, past_rows, TM_FFN, TF_FFN)
        ffn_p.append(tail_p[:, -1, SUBLANES - (CONV_W - 1):])
        ffn_s.append(a_s.reshape(db, T_PAD, dff)[:, t - (CONV_W - 1):t])
    ys = xs.reshape(db, T_PAD, d)[:, :t]
    return (xp, ys, jnp.stack(mla_p), jnp.stack(mla_s), jnp.stack(sb_p), jnp.stack(sb_s),
            jnp.stack(hg_p), jnp.stack(hg_s), jnp.stack(ffn_p), jnp.stack(ffn_s))
```

```python
import functools

import jax
import jax.numpy as jnp
import numpy as np
from jax import lax
from jax.experimental import pallas as pl
from jax.experimental.pallas import tpu as pltpu

F32 = jnp.float32
BF16 = jnp.bfloat16

N_MIXERS = 3
MLA_HEADS = 8
QK_NOPE = 128
QK_ROPE = 64
V_HEAD = 128
Q_LORA = 256
KV_LORA = 256
QK_HEAD = QK_NOPE + QK_ROPE
MLA_SCALE = QK_HEAD ** -0.5
ROPE_BASE = 10000.0
SB_HEADS = 8
SB_HEAD = 128
SB_SCALE = SB_HEAD ** -0.5
HG_HEADS = 8
HG_DK = 128
HG_DV = 128
HG_CHUNK = 32
CONV_W = 3
PAGE_SIZE = 128
EPS = 1e-6
N_MOD = 6

LANES = 128
SUBLANES = 8
T_PAD = SUBLANES
NEG_BIG = -1e30
HEAD_SHIFT = 3
LANE_SHIFT = 7
VMEM_LIMIT = 48 * 1024 * 1024


def _cparams(sem):
    return pltpu.CompilerParams(dimension_semantics=sem, vmem_limit_bytes=VMEM_LIMIT)


def _dot(a, b):
    return jnp.dot(a, b, preferred_element_type=F32)


def _dot_nt(a, b):
    return lax.dot_general(a, b, (((1,), (1,)), ((), ())), preferred_element_type=F32)


def _dot_tn(a, b):
    return lax.dot_general(a, b, (((0,), (0,)), ((), ())), preferred_element_type=F32)


def _split_bf16(x, parts):
    out = []
    r = x
    for _ in range(parts):
        p = r.astype(BF16)
        out.append(p)
        r = r - p.astype(F32)
    return out


def _dot01(m01, x, parts=3):
    acc = None
    for p in _split_bf16(x, parts):
        d = _dot(m01, p)
        acc = d if acc is None else acc + d
    return acc


def _dot01_right(x, m01, parts=2):
    acc = None
    for p in _split_bf16(x, parts):
        d = _dot(p, m01)
        acc = d if acc is None else acc + d
    return acc


def _rms(x, g):
    ms = jnp.mean(x * x, axis=-1, keepdims=True)
    return x * lax.rsqrt(ms + EPS) * g


def _norm_mod(x, g, shift, scale):
    return _rms(x, g) * (1.0 + scale) + shift


def _silu(x):
    return x * jax.nn.sigmoid(x)


def _mod_kernel(c_ref, w_ref, b_ref, o_ref):
    s = _silu(c_ref[...])
    o_ref[...] = _dot(s.astype(BF16), w_ref[...].astype(BF16)) + b_ref[...]


def ada_mod(c_all, w_mod, b_mod):
    depth, d, n = w_mod.shape
    nc = c_all.shape[0]
    tn = n // 4
    return pl.pallas_call(
        _mod_kernel,
        out_shape=jax.ShapeDtypeStruct((depth, nc, n), F32),
        grid=(depth, n // tn),
        in_specs=[
            pl.BlockSpec((nc, d), lambda l, j: (0, 0)),
            pl.BlockSpec((None, d, tn), lambda l, j: (l, 0, j)),
            pl.BlockSpec((None, 1, tn), lambda l, j: (l, 0, j)),
        ],
        out_specs=pl.BlockSpec((None, nc, tn), lambda l, j: (l, 0, j)),
        compiler_params=_cparams(("arbitrary", "arbitrary")),
        name="ada_mod",
    )(c_all, w_mod, b_mod.reshape(depth, 1, n))


def _mod_spec(mod, tm):
    r = mod.shape[1]
    if r == 1:
        return pl.BlockSpec((None, 1, mod.shape[2]), lambda b, i, *_: (b, 0, 0))
    return pl.BlockSpec((None, tm, mod.shape[2]), lambda b, i, *_: (b, i, 0))


def _nml_kernel(x_ref, g_ref, sh_ref, sc_ref, *rest, n_w):
    h = _norm_mod(x_ref[...], g_ref[...], sh_ref[...], sc_ref[...]).astype(BF16)
    for w_ref, o_ref in zip(rest[:n_w], rest[n_w:]):
        o_ref[...] = _dot(h, w_ref[...]).astype(o_ref.dtype)


def norm_mod_linear(x, g, shift, scale, ws, tm):
    nb, s, d = x.shape
    tm = min(tm, s)
    in_specs = [
        pl.BlockSpec((None, tm, d), lambda b, i: (b, i, 0)),
        pl.BlockSpec((1, d), lambda b, i: (0, 0)),
        _mod_spec(shift, tm),
        _mod_spec(scale, tm),
    ]
    out_shape, out_specs = [], []
    for w in ws:
        in_specs.append(pl.BlockSpec(w.shape, lambda b, i: (0, 0)))
        out_shape.append(jax.ShapeDtypeStruct((nb, s, w.shape[1]), F32))
        out_specs.append(pl.BlockSpec((None, tm, w.shape[1]), lambda b, i: (b, i, 0)))
    return pl.pallas_call(
        functools.partial(_nml_kernel, n_w=len(ws)),
        out_shape=out_shape,
        grid=(nb, s // tm),
        in_specs=in_specs,
        out_specs=out_specs,
        compiler_params=_cparams(("arbitrary", "arbitrary")),
        name="norm_mod_linear",
    )(x, g.reshape(1, d), shift, scale, *ws)


def _proj_res_kernel(o_ref, w_ref, x_ref, gate_ref, out_ref):
    y = _dot(o_ref[...].astype(BF16), w_ref[...])
    out_ref[...] = x_ref[...] + gate_ref[...] * y


def proj_residual(o, w, x, gate, tm):
    nb, s, d = x.shape
    din = o.shape[2]
    tm = min(tm, s)
    return pl.pallas_call(
        _proj_res_kernel,
        out_shape=jax.ShapeDtypeStruct((nb, s, d), F32),
        grid=(nb, s // tm),
        in_specs=[
            pl.BlockSpec((None, tm, din), lambda b, i: (b, i, 0)),
            pl.BlockSpec((din, d), lambda b, i: (0, 0)),
            pl.BlockSpec((None, tm, d), lambda b, i: (b, i, 0)),
            _mod_spec(gate, tm),
        ],
        out_specs=pl.BlockSpec((None, tm, d), lambda b, i: (b, i, 0)),
        compiler_params=_cparams(("arbitrary", "arbitrary")),
        name="proj_residual",
    )(o, w, x, gate)


def _mla_proj_kernel(x_ref, g_ref, sh_ref, sc_ref, cos_ref, sin_ref, wa_ref, gqa_ref, gkva_ref,
                     wqm_ref, wqs_ref, wkv_ref, gq_ref, gk_ref,
                     rows_ref, q_ref, k_ref, v_ref):
    h = _norm_mod(x_ref[...], g_ref[...], sh_ref[...], sc_ref[...]).astype(BF16)
    a = _dot(h, wa_ref[...])
    cos = cos_ref[...]
    sin = sin_ref[...]
    qn = _rms(a[:, :Q_LORA], gqa_ref[...]).astype(BF16)
    cn = _rms(a[:, Q_LORA:Q_LORA + KV_LORA], gkva_ref[...])
    o = Q_LORA + KV_LORA
    kpe = a[:, o:o + LANES] * cos + a[:, o + LANES:o + 2 * LANES] * sin
    rows_ref[...] = jnp.concatenate([cn, kpe[:, :QK_ROPE]], axis=-1)
    qm = _dot(qn, wqm_ref[...])
    qs = _dot(qn, wqs_ref[...])
    kv = _dot(cn.astype(BF16), wkv_ref[...])
    gq = gq_ref[...]
    gk = gk_ref[...]
    kpe_ss = jnp.sum(kpe * kpe, axis=-1, keepdims=True)
    inv = 1.0 / QK_HEAD
    for hd in range(MLA_HEADS):
        c0 = 2 * LANES * hd
        nope = qm[:, c0:c0 + LANES]
        pe = qm[:, c0 + LANES:c0 + 2 * LANES] * cos + qs[:, LANES * hd:LANES * (hd + 1)] * sin
        ss = jnp.sum(nope * nope, axis=-1, keepdims=True) + jnp.sum(pe * pe, axis=-1, keepdims=True)
        r = lax.rsqrt(ss * inv + EPS) * MLA_SCALE
        q_ref[:, c0:c0 + LANES] = (nope * r * gq[:, :LANES]).astype(q_ref.dtype)
        q_ref[:, c0 + LANES:c0 + 2 * LANES] = (pe * r * gq[:, LANES:]).astype(q_ref.dtype)
        kn = kv[:, c0:c0 + LANES]
        rk = lax.rsqrt((jnp.sum(kn * kn, axis=-1, keepdims=True) + kpe_ss) * inv + EPS)
        k_ref[:, c0:c0 + LANES] = (kn * rk * gk[:, :LANES]).astype(k_ref.dtype)
        k_ref[:, c0 + LANES:c0 + 2 * LANES] = (kpe * rk * gk[:, LANES:]).astype(k_ref.dtype)
        v_ref[:, LANES * hd:LANES * (hd + 1)] = kv[:, c0 + LANES:c0 + 2 * LANES].astype(v_ref.dtype)


def mla_proj(x, g, shift, scale, cos, sin, wp, q_dtype, tm):
    nb, s, d = x.shape
    tm = min(tm, s)
    hw = 2 * LANES * MLA_HEADS
    full = lambda arr: pl.BlockSpec(arr.shape, lambda b, i: (0, 0))
    return pl.pallas_call(
        _mla_proj_kernel,
        out_shape=[
            jax.ShapeDtypeStruct((nb, s, KV_LORA + QK_ROPE), F32),
            jax.ShapeDtypeStruct((nb, s, hw), q_dtype),
            jax.ShapeDtypeStruct((nb, s, hw), BF16),
            jax.ShapeDtypeStruct((nb, s, LANES * MLA_HEADS), BF16),
        ],
        grid=(nb, s // tm),
        in_specs=[
            pl.BlockSpec((None, tm, d), lambda b, i: (b, i, 0)),
            pl.BlockSpec((1, d), lambda b, i: (0, 0)),
            _mod_spec(shift, tm),
            _mod_spec(scale, tm),
            pl.BlockSpec((tm, LANES), lambda b, i: (i, 0)),
            pl.BlockSpec((tm, LANES), lambda b, i: (i, 0)),
            full(wp["wa"]), full(wp["gqa"]), full(wp["gkva"]), full(wp["wqm"]), full(wp["wqs"]),
            full(wp["wkv"]), full(wp["gq"]), full(wp["gk"]),
        ],
        out_specs=[
            pl.BlockSpec((None, tm, KV_LORA + QK_ROPE), lambda b, i: (b, i, 0)),
            pl.BlockSpec((None, tm, hw), lambda b, i: (b, i, 0)),
            pl.BlockSpec((None, tm, hw), lambda b, i: (b, i, 0)),
            pl.BlockSpec((None, tm, LANES * MLA_HEADS), lambda b, i: (b, i, 0)),
        ],
        compiler_params=_cparams(("arbitrary", "arbitrary")),
        name="mla_proj",
    )(x, g.reshape(1, d), shift, scale, cos, sin, wp["wa"], wp["gqa"], wp["gkva"], wp["wqm"], wp["wqs"],
      wp["wkv"], wp["gq"], wp["gk"])


def _mla_weights(w_a, g_qa, g_kva, w_uq, w_ukv, g_q, g_k):
    d = w_a.shape[0]
    half = QK_ROPE // 2
    zpad = jnp.zeros((d, LANES - QK_ROPE), F32)
    w_pe = w_a[:, Q_LORA + KV_LORA:]
    w_pe_sw = jnp.concatenate([w_pe[:, half:], w_pe[:, :half]], axis=1)
    wa = jnp.concatenate([w_a[:, :Q_LORA + KV_LORA], w_pe, zpad, w_pe_sw, zpad], axis=1).astype(BF16)
    wq = w_uq.reshape(Q_LORA, MLA_HEADS, QK_HEAD)
    wq_pe = wq[:, :, QK_NOPE:]
    z = jnp.zeros((Q_LORA, MLA_HEADS, LANES - QK_ROPE), F32)
    wqm = jnp.concatenate([wq, z], axis=2).reshape(Q_LORA, MLA_HEADS * 2 * LANES).astype(BF16)
    wq_sw = jnp.concatenate([wq_pe[:, :, half:], wq_pe[:, :, :half], z], axis=2)
    wqs = wq_sw.reshape(Q_LORA, MLA_HEADS * LANES).astype(BF16)
    gpad = jnp.zeros((2 * LANES - QK_HEAD,), F32)
    wkv3 = w_ukv.reshape(KV_LORA, MLA_HEADS, QK_NOPE + V_HEAD)
    return {
        "wa": wa,
        "gqa": g_qa.reshape(1, Q_LORA),
        "gkva": g_kva.reshape(1, KV_LORA),
        "wqm": wqm,
        "wqs": wqs,
        "wkv": w_ukv.astype(BF16),
        "gq": jnp.concatenate([g_q, gpad]).reshape(1, 2 * LANES),
        "gk": jnp.concatenate([g_k, gpad]).reshape(1, 2 * LANES),
        "wk_t": wkv3[:, :, :QK_NOPE].reshape(KV_LORA, MLA_HEADS * QK_NOPE).T.astype(BF16),
        "wk": wkv3[:, :, :QK_NOPE].reshape(KV_LORA, MLA_HEADS * QK_NOPE).astype(BF16),
        "wv": wkv3[:, :, QK_NOPE:].reshape(KV_LORA, MLA_HEADS * V_HEAD).astype(BF16),
    }


def _rope_tables(pos):
    half = QK_ROPE // 2
    inv = ROPE_BASE ** (-jnp.arange(half, dtype=F32) / half)
    ang = pos.astype(F32)[:, None] * inv[None, :]
    cos, sin = jnp.cos(ang), jnp.sin(ang)
    z = jnp.zeros((pos.shape[0], LANES - QK_ROPE), F32)
    return jnp.concatenate([cos, cos, z], axis=1), jnp.concatenate([-sin, sin, z], axis=1)


def _pairs(n_q, tq, tk, newest_first):
    qi, ki = [], []
    for a in range(n_q):
        last = ((a + 1) * tq - 1) // tk
        ks = range(last, -1, -1) if newest_first else range(last + 1)
        for b in ks:
            qi.append(a)
            ki.append(b)
    return jnp.asarray(qi, jnp.int32), jnp.asarray(ki, jnp.int32)


def _flash_kernel(qi_ref, ki_ref, q_ref, k_ref, v_ref, o_ref, m_sc, l_sc, acc_sc, *, tq, tk):
    p = pl.program_id(2)
    qi = qi_ref[p]
    ki = ki_ref[p]

    @pl.when(ki == 0)
    def _():
        m_sc[...] = jnp.full(m_sc.shape, NEG_BIG, F32)
        l_sc[...] = jnp.zeros(l_sc.shape, F32)
        acc_sc[...] = jnp.zeros(acc_sc.shape, F32)

    def update(masked):
        s = _dot_nt(q_ref[...], k_ref[...])
        if masked:
            row = qi * tq + lax.broadcasted_iota(jnp.int32, (tq, tk), 0)
            col = ki * tk + lax.broadcasted_iota(jnp.int32, (tq, tk), 1)
            s = jnp.where(row >= col, s, NEG_BIG)
        m_prev = m_sc[...]
        m_new = jnp.maximum(m_prev, jnp.max(s, axis=-1, keepdims=True))
        alpha = jnp.exp(m_prev - m_new)
        pr = jnp.exp(s - m_new)
        l_sc[...] = alpha * l_sc[...] + jnp.sum(pr, axis=-1, keepdims=True)
        acc_sc[...] = alpha * acc_sc[...] + _dot(pr.astype(BF16), v_ref[...])
        m_sc[...] = m_new

    on_diagonal = (ki + 1) * tk > qi * tq

    @pl.when(on_diagonal)
    def _():
        update(True)

    @pl.when(jnp.logical_not(on_diagonal))
    def _():
        update(False)

    @pl.when(ki == ((qi + 1) * tq - 1) // tk)
    def _():
        o_ref[...] = (acc_sc[...] / l_sc[...]).astype(o_ref.dtype)


def flash_mla(q, k, v, tq, tk):
    nb, s, _ = q.shape
    tq, tk = min(tq, s), min(tk, s)
    qi, ki = _pairs(s // tq, tq, tk, newest_first=False)
    grid_spec = pltpu.PrefetchScalarGridSpec(
        num_scalar_prefetch=2,
        grid=(nb, MLA_HEADS, qi.shape[0]),
        in_specs=[
            pl.BlockSpec((None, tq, 2 * LANES), lambda b, h, p, qi, ki: (b, qi[p], h)),
            pl.BlockSpec((None, tk, 2 * LANES), lambda b, h, p, qi, ki: (b, ki[p], h)),
            pl.BlockSpec((None, tk, LANES), lambda b, h, p, qi, ki: (b, ki[p], h)),
        ],
        out_specs=pl.BlockSpec((None, tq, LANES), lambda b, h, p, qi, ki: (b, qi[p], h)),
        scratch_shapes=[pltpu.VMEM((tq, 1), F32), pltpu.VMEM((tq, 1), F32), pltpu.VMEM((tq, LANES), F32)],
    )
    return pl.pallas_call(
        functools.partial(_flash_kernel, tq=tq, tk=tk),
        out_shape=jax.ShapeDtypeStruct((nb, s, MLA_HEADS * LANES), BF16),
        grid_spec=grid_spec,
        compiler_params=_cparams(("arbitrary", "arbitrary", "arbitrary")),
        name="flash_mla",
    )(qi, ki, q, k, v)


def _sb_terms(z):
    tail = jnp.log1p(jnp.exp(-jnp.abs(z)))
    return -(jnp.maximum(z, 0.0) + tail), jnp.minimum(z, 0.0) - tail


def _sb_kernel(qi_ref, ki_ref, q_ref, k_ref, v_ref, u_ref, o_ref, keep_sc, acc_sc, *, tq, tk):
    p = pl.program_id(2)
    qi = qi_ref[p]
    ki = ki_ref[p]

    @pl.when(ki == ((qi + 1) * tq - 1) // tk)
    def _():
        keep_sc[...] = jnp.zeros(keep_sc.shape, F32)
        acc_sc[...] = jnp.zeros(acc_sc.shape, F32)

    def update(masked):
        q = (q_ref[...] * SB_SCALE).astype(BF16)
        z = _dot_nt(q, k_ref[...].astype(BF16))
        log_keep, log_break = _sb_terms(z)
        if masked:
            row = qi * tq + lax.broadcasted_iota(jnp.int32, (tq, tk), 0)
            col = ki * tk + lax.broadcasted_iota(jnp.int32, (tq, tk), 1)
            mask = row > col
            log_keep = jnp.where(mask, log_keep, 0.0)
        a = jnp.exp(log_break + _dot01_right(log_keep, u_ref[...]) + keep_sc[...])
        if masked:
            a = jnp.where(mask, a, 0.0)
        acc_sc[...] += _dot(a.astype(BF16), v_ref[...].astype(BF16))
        keep_sc[...] += jnp.sum(log_keep, axis=-1, keepdims=True)

    on_diagonal = (ki + 1) * tk > qi * tq

    @pl.when(on_diagonal)
    def _():
        update(True)

    @pl.when(jnp.logical_not(on_diagonal))
    def _():
        update(False)

    @pl.when(ki == 0)
    def _():
        o_ref[...] = acc_sc[...].astype(o_ref.dtype)


def _later_matrix(n):
    return (lax.broadcasted_iota(jnp.int32, (n, n), 0) > lax.broadcasted_iota(jnp.int32, (n, n), 1)).astype(BF16)


def sb_attention(q, kv, tq, tk):
    nb, s, _ = q.shape
    tq, tk = min(tq, s), min(tk, s)
    qi, ki = _pairs(s // tq, tq, tk, newest_first=True)
    grid_spec = pltpu.PrefetchScalarGridSpec(
        num_scalar_prefetch=2,
        grid=(nb, SB_HEADS, qi.shape[0]),
        in_specs=[
            pl.BlockSpec((None, tq, SB_HEAD), lambda b, h, p, qi, ki: (b, qi[p], h)),
            pl.BlockSpec((None, tk, SB_HEAD), lambda b, h, p, qi, ki: (b, ki[p], h)),
            pl.BlockSpec((None, tk, SB_HEAD), lambda b, h, p, qi, ki: (b, ki[p], SB_HEADS + h)),
            pl.BlockSpec((tk, tk), lambda b, h, p, qi, ki: (0, 0)),
        ],
        out_specs=pl.BlockSpec((None, tq, SB_HEAD), lambda b, h, p, qi, ki: (b, qi[p], h)),
        scratch_shapes=[pltpu.VMEM((tq, 1), F32), pltpu.VMEM((tq, SB_HEAD), F32)],
    )
    return pl.pallas_call(
        functools.partial(_sb_kernel, tq=tq, tk=tk),
        out_shape=jax.ShapeDtypeStruct((nb, s, SB_HEADS * SB_HEAD), BF16),
        grid_spec=grid_spec,
        compiler_params=_cparams(("arbitrary", "arbitrary", "arbitrary")),
        name="sb_attention",
    )(qi, ki, q, kv, kv, _later_matrix(tk))


def _hgrn_kernel(x_ref, lbl_ref, st0_ref, go_ref, mc_ref, ml_ref, o_ref, stf_ref,
                 st_sc, qd_sc, kd_sc, kl_sc, el_sc, *, layer, chunk, t_valid, ts):
    s = pl.program_id(1)
    hk = HG_HEADS * HG_DK
    cdt = BF16 if chunk % 16 == 0 else F32

    @pl.when(s == 0)
    def _():
        for hd in range(HG_HEADS):
            st_sc[hd] = st0_ref[hd].T

    logits = lbl_ref[...]
    e = jnp.exp(logits - jnp.max(logits, axis=0, keepdims=True))
    sm = e / jnp.sum(e, axis=0, keepdims=True)
    lb = jnp.sum(sm[1:layer + 1], axis=0, keepdims=True)
    q = x_ref[:, :hk]
    forget = lb + (1.0 - lb) * jax.nn.sigmoid(x_ref[:, hk:2 * hk])
    k = 1.0 - forget
    log_f = jnp.log(forget)
    if t_valid < ts:
        valid = lax.broadcasted_iota(jnp.int32, (ts, hk), 0) < t_valid
        log_f = jnp.where(valid, log_f, 0.0)
        k = jnp.where(valid, k, 0.0)
    if cdt == BF16:
        cum = _dot01(mc_ref[...], log_f)
        last = _dot01(ml_ref[...], log_f)
    else:
        cum = jnp.dot(mc_ref[...], log_f, precision=lax.Precision.HIGHEST, preferred_element_type=F32)
        last = jnp.dot(ml_ref[...], log_f, precision=lax.Precision.HIGHEST, preferred_element_type=F32)
    qd_sc[...] = q * jnp.exp(cum)
    kd_sc[...] = k * jnp.exp(-cum)
    kl_sc[...] = k * jnp.exp(last - cum)
    el_sc[...] = jnp.exp(last)
    go = go_ref[...]
    causal = (lax.broadcasted_iota(jnp.int32, (chunk, chunk), 0)
              >= lax.broadcasted_iota(jnp.int32, (chunk, chunk), 1))

    def chunk_body(c, carry):
        r0 = pl.multiple_of(c * chunk, chunk)
        rows = pl.ds(r0, chunk)
        for hd in range(HG_HEADS):
            cols = slice(HG_DK * hd, HG_DK * (hd + 1))
            qd = qd_sc[rows, cols].astype(cdt)
            kd = kd_sc[rows, cols].astype(cdt)
            kl = kl_sc[rows, cols].astype(cdt)
            v = x_ref[rows, 2 * hk + HG_DV * hd:2 * hk + HG_DV * (hd + 1)].astype(cdt)
            gate = x_ref[rows, 2 * hk + HG_HEADS * HG_DV + HG_DV * hd:2 * hk + HG_HEADS * HG_DV + HG_DV * (hd + 1)]
            el = el_sc[pl.ds(r0, 1), cols]
            st = st_sc[hd]
            att = jnp.where(causal, _dot_nt(qd, kd), 0.0)
            o = _dot_nt(qd, st.astype(cdt)) + _dot(att.astype(cdt), v)
            st_sc[hd] = st * el + _dot_tn(v, kl)
            on = _rms(o, go) * _silu(gate)
            o_ref[rows, HG_DV * hd:HG_DV * (hd + 1)] = on.astype(o_ref.dtype)
        return carry

    lax.fori_loop(0, ts // chunk, chunk_body, 0)

    @pl.when(s == pl.num_programs(1) - 1)
    def _():
        for hd in range(HG_HEADS):
            stf_ref[hd] = st_sc[hd].T


def _chunk_matrices(ts, chunk):
    r = lax.broadcasted_iota(jnp.int32, (ts, ts), 0)
    c = lax.broadcasted_iota(jnp.int32, (ts, ts), 1)
    same = (r // chunk) == (c // chunk)
    dtype = BF16 if chunk % 16 == 0 else F32
    return (same & (c <= r)).astype(dtype), same.astype(dtype)


def hgrn_scan(x, lb_logits, state0, g_o, layer, chunk, t_valid, ts, out_dtype):
    nb, s, _ = x.shape
    ts = min(ts, s)
    hk, hv = HG_HEADS * HG_DK, HG_HEADS * HG_DV
    mc, ml = _chunk_matrices(ts, chunk)
    return pl.pallas_call(
        functools.partial(_hgrn_kernel, layer=layer, chunk=chunk, t_valid=t_valid, ts=ts),
        out_shape=[
            jax.ShapeDtypeStruct((nb, s, hv), out_dtype),
            jax.ShapeDtypeStruct((nb, HG_HEADS, HG_DK, HG_DV), F32),
        ],
        grid=(nb, s // ts),
        in_specs=[
            pl.BlockSpec((None, ts, 2 * hk + 2 * hv), lambda b, i: (b, i, 0)),
            pl.BlockSpec(lb_logits.shape, lambda b, i: (0, 0)),
            pl.BlockSpec((None, HG_HEADS, HG_DK, HG_DV), lambda b, i: (b, 0, 0, 0)),
            pl.BlockSpec((1, HG_DV), lambda b, i: (0, 0)),
            pl.BlockSpec((ts, ts), lambda b, i: (0, 0)),
            pl.BlockSpec((ts, ts), lambda b, i: (0, 0)),
        ],
        out_specs=[
            pl.BlockSpec((None, ts, hv), lambda b, i: (b, i, 0)),
            pl.BlockSpec((None, HG_HEADS, HG_DK, HG_DV), lambda b, i: (b, 0, 0, 0)),
        ],
        scratch_shapes=[
            pltpu.VMEM((HG_HEADS, HG_DV, HG_DK), F32),
            pltpu.VMEM((ts, hk), F32),
            pltpu.VMEM((ts, hk), F32),
            pltpu.VMEM((ts, hk), F32),
            pltpu.VMEM((ts, hk), F32),
        ],
        compiler_params=_cparams(("arbitrary", "arbitrary")),
        name="hgrn_scan",
    )(x, lb_logits, state0, g_o.reshape(1, HG_DV), mc, ml)


def _ffn_kernel(*refs, tm, seg, stream):
    if stream:
        (x_ref, g_ref, sh_ref, sc_ref, gate_ref, wa_ref, wu_ref, cw_ref, cb_ref, wd_ref,
         y_ref, tail_ref, h_sc, acc_sc, carry_sc) = refs
    else:
        (x_ref, g_ref, sh_ref, sc_ref, gate_ref, wa_ref, wu_ref, cw_ref, cb_ref, wd_ref, past_ref,
         y_ref, tail_ref, h_sc, acc_sc) = refs
    i = pl.program_id(1)
    f = pl.program_id(2)

    @pl.when(f == 0)
    def _():
        h_sc[...] = _norm_mod(x_ref[...], g_ref[...], sh_ref[...], sc_ref[...]).astype(BF16)
        acc_sc[...] = jnp.zeros(acc_sc.shape, F32)

    h = h_sc[...]
    a = _dot(h, wa_ref[...])
    u = _dot(h, wu_ref[...])
    tf = a.shape[1]
    row = lax.broadcasted_iota(jnp.int32, (tm, tf), 0)
    a1 = pltpu.roll(a, 1, 0)
    a2 = pltpu.roll(a, 2, 0)
    if stream:
        @pl.when(i == 0)
        def _():
            carry_sc[f] = jnp.zeros(carry_sc.shape[1:], F32)

        c = carry_sc[f]
        a1 = jnp.where(row == 0, c[SUBLANES - 1:SUBLANES], a1)
        a2 = jnp.where(row == 0, c[SUBLANES - 2:SUBLANES - 1], jnp.where(row == 1, c[SUBLANES - 1:SUBLANES], a2))
        carry_sc[f] = a[tm - SUBLANES:, :]
        tail_ref[...] = a[tm - SUBLANES:, :]
    else:
        t = jnp.bitwise_and(row, seg - 1)
        a1 = jnp.where(t == 0, past_ref[1], a1)
        a2 = jnp.where(t < 2, past_ref[0], a2)
        tail_ref[...] = a
    cw = cw_ref[...]
    conv = cb_ref[...] + a2 * cw[0:1] + a1 * cw[1:2] + a * cw[2:3]
    mid = (_silu(conv) * u).astype(BF16)
    acc_sc[...] += _dot(mid, wd_ref[...])

    @pl.when(f == pl.num_programs(2) - 1)
    def _():
        y_ref[...] = x_ref[...] + gate_ref[...] * acc_sc[...]


def conv_ffn(x, g, shift, scale, gate, w_up, conv_w, conv_b, w_down, past, tm, tf):
    nb, s, d = x.shape
    dff = w_down.shape[0]
    tm = min(tm, s)
    nf = dff // tf
    stream = past is None
    in_specs = [
        pl.BlockSpec((None, tm, d), lambda b, i, f: (b, i, 0)),
        pl.BlockSpec((1, d), lambda b, i, f: (0, 0)),
        _mod_spec(shift, tm),
        _mod_spec(scale, tm),
        _mod_spec(gate, tm),
        pl.BlockSpec((d, tf), lambda b, i, f: (0, f)),
        pl.BlockSpec((d, tf), lambda b, i, f: (0, nf + f)),
        pl.BlockSpec((CONV_W, tf), lambda b, i, f: (0, f)),
        pl.BlockSpec((1, tf), lambda b, i, f: (0, f)),
        pl.BlockSpec((tf, d), lambda b, i, f: (f, 0)),
    ]
    args = [x, g.reshape(1, d), shift, scale, gate, w_up, w_up, conv_w, conv_b.reshape(1, dff), w_down]
    scratch = [pltpu.VMEM((tm, d), BF16), pltpu.VMEM((tm, d), F32)]
    if stream:
        tail_shape = jax.ShapeDtypeStruct((nb, s // tm, SUBLANES, dff), F32)
        tail_spec = pl.BlockSpec((None, None, SUBLANES, tf), lambda b, i, f: (b, i, 0, f))
        scratch.append(pltpu.VMEM((nf, SUBLANES, tf), F32))
    else:
        in_specs.append(pl.BlockSpec((CONV_W - 1, tm, tf), lambda b, i, f: (0, i, f)))
        args.append(past)
        tail_shape = jax.ShapeDtypeStruct((nb, s, dff), F32)
        tail_spec = pl.BlockSpec((None, tm, tf), lambda b, i, f: (b, i, f))
    return pl.pallas_call(
        functools.partial(_ffn_kernel, tm=tm, seg=T_PAD, stream=stream),
        out_shape=[jax.ShapeDtypeStruct((nb, s, d), F32), tail_shape],
        grid=(nb, s // tm, nf),
        in_specs=in_specs,
        out_specs=[pl.BlockSpec((None, tm, d), lambda b, i, f: (b, i, 0)), tail_spec],
        scratch_shapes=scratch,
        compiler_params=_cparams(("arbitrary", "arbitrary", "arbitrary")),
        name="conv_ffn",
    )(*args)


def _absorb_kernel(q_ref, wk_ref, gk_ref, qa_ref, qp_ref):
    gk = gk_ref[...]
    for hd in range(MLA_HEADS):
        c0 = 2 * LANES * hd
        qn = (q_ref[:, c0:c0 + LANES] * gk[:, :LANES]).astype(BF16)
        qa_ref[:, c0:c0 + 2 * LANES] = _dot_nt(qn, wk_ref[:, LANES * hd:LANES * (hd + 1)]).astype(qa_ref.dtype)
        qp_ref[:, LANES * hd:LANES * (hd + 1)] = (q_ref[:, c0 + LANES:c0 + 2 * LANES] * gk[:, LANES:]).astype(qp_ref.dtype)


def mla_absorb(q, wk, gk):
    m = q.shape[0]
    return pl.pallas_call(
        _absorb_kernel,
        out_shape=[jax.ShapeDtypeStruct((m, MLA_HEADS * KV_LORA), BF16),
                   jax.ShapeDtypeStruct((m, MLA_HEADS * LANES), BF16)],
        name="mla_absorb",
    )(q, wk, gk)


def _vup_kernel(o_ref, wv_ref, out_ref):
    for hd in range(MLA_HEADS):
        lat = o_ref[:, KV_LORA * hd:KV_LORA * (hd + 1)].astype(BF16)
        out_ref[:, V_HEAD * hd:V_HEAD * (hd + 1)] = _dot(lat, wv_ref[:, V_HEAD * hd:V_HEAD * (hd + 1)])


def mla_vup(o_lat, wv):
    m = o_lat.shape[0]
    return pl.pallas_call(
        _vup_kernel,
        out_shape=jax.ShapeDtypeStruct((m, MLA_HEADS * V_HEAD), F32),
        name="mla_vup",
    )(o_lat, wv)


def _mla_dec_kernel(pt_ref, new_ref, *refs, n_slots):
    page_refs = refs[:n_slots]
    qa_ref, qp_ref, wkt_ref, o_ref, m_sc, l_sc, acc_sc = refs[n_slots:]
    s = pl.program_id(1)
    n_rows = T_PAD * MLA_HEADS

    def scores(tile):
        cb = tile[:KV_LORA].astype(BF16)
        kpt = tile[KV_LORA:]
        kt = _dot(wkt_ref[...], cb)
        ss_pe = jnp.sum(kpt * kpt, axis=0, keepdims=True)
        ss = jnp.concatenate(
            [jnp.sum(kt[QK_NOPE * hd:QK_NOPE * (hd + 1)] ** 2, axis=0, keepdims=True) for hd in range(MLA_HEADS)],
            axis=0)
        r = lax.rsqrt((ss + ss_pe) * (1.0 / QK_HEAD) + EPS)
        sc = _dot(qa_ref[...], cb) + _dot(qp_ref[...], kpt.astype(BF16))
        return sc * jnp.concatenate([r] * T_PAD, axis=0), cb

    def update(scs, cbs):
        m_prev = m_sc[...]
        m_new = m_prev
        for sc in scs:
            m_new = jnp.maximum(m_new, jnp.max(sc, axis=-1, keepdims=True))
        alpha = jnp.exp(m_prev - m_new)
        l_new = alpha * l_sc[...]
        acc = alpha * acc_sc[...]
        for sc, cb in zip(scs, cbs):
            pr = jnp.exp(sc - m_new)
            l_new = l_new + jnp.sum(pr, axis=-1, keepdims=True)
            acc = acc + _dot_nt(pr.astype(BF16), cb)
        l_sc[...] = l_new
        acc_sc[...] = acc
        m_sc[...] = m_new

    @pl.when(s == 0)
    def _():
        m_sc[...] = jnp.full((n_rows, 1), NEG_BIG, F32)
        l_sc[...] = jnp.zeros((n_rows, 1), F32)
        acc_sc[...] = jnp.zeros((n_rows, KV_LORA), F32)
        sc, cb = scores(new_ref[...])
        tok = lax.shift_right_logical(lax.broadcasted_iota(jnp.int32, sc.shape, 0), HEAD_SHIFT)
        key = lax.broadcasted_iota(jnp.int32, sc.shape, 1)
        update([jnp.where(key <= tok, sc, NEG_BIG)], [cb])

    @pl.when(s > 0)
    def _():
        pairs = [scores(jnp.concatenate([page_refs[a][...], page_refs[a + 1][...]], axis=1))
                 for a in range(0, n_slots, 2)]
        update([p[0] for p in pairs], [p[1] for p in pairs])

    @pl.when(s == pl.num_programs(1) - 1)
    def _():
        o_ref[...] = acc_sc[...] / l_sc[...]


def mla_decode(page_table, new_page, cache_t, layer, qa, qp, wk_t, n_slots):
    db, n_pages = page_table.shape
    n_rows = T_PAD * MLA_HEADS
    row_w = cache_t.shape[2]
    assert n_slots % 2 == 0 and n_pages % n_slots == 0
    steps = n_pages // n_slots

    def page_spec(slot):
        def imap(b, s, pt):
            idx = jnp.maximum(s - 1, 0) * n_slots + slot
            return (layer, pt[b * n_pages + idx], 0, 0)
        return pl.BlockSpec((None, None, row_w, PAGE_SIZE), imap)

    grid_spec = pltpu.PrefetchScalarGridSpec(
        num_scalar_prefetch=1,
        grid=(db, steps + 1),
        in_specs=[pl.BlockSpec((None, row_w, PAGE_SIZE), lambda b, s, pt: (b, 0, 0))]
        + [page_spec(i) for i in range(n_slots)]
        + [
            pl.BlockSpec((None, n_rows, KV_LORA), lambda b, s, pt: (b, 0, 0)),
            pl.BlockSpec((None, n_rows, QK_ROPE), lambda b, s, pt: (b, 0, 0)),
            pl.BlockSpec(wk_t.shape, lambda b, s, pt: (0, 0)),
        ],
        out_specs=pl.BlockSpec((None, n_rows, KV_LORA), lambda b, s, pt: (b, 0, 0)),
        scratch_shapes=[pltpu.VMEM((n_rows, 1), F32), pltpu.VMEM((n_rows, 1), F32),
                        pltpu.VMEM((n_rows, KV_LORA), F32)],
    )
    return pl.pallas_call(
        functools.partial(_mla_dec_kernel, n_slots=n_slots),
        out_shape=jax.ShapeDtypeStruct((db, n_rows, KV_LORA), F32),
        grid_spec=grid_spec,
        compiler_params=_cparams(("arbitrary", "arbitrary")),
        name="mla_decode",
    )(page_table.reshape(-1), new_page, *([cache_t] * n_slots), qa, qp, wk_t)


def _sb_dec_kernel(pt_ref, new_ref, *refs, n_slots):
    page_refs = refs[:n_slots]
    q_ref, u_ref, un_ref, shn_ref, o_ref, keep_sc, acc_sc = refs[n_slots:]
    s = pl.program_id(1)
    n_rows = T_PAD * SB_HEADS

    def process(pages, n_keys, causal, cdt, keep, acc, later_of, shift_of):
        w = 2 * n_keys
        qs = [(q_ref[:, SB_HEAD * hd:SB_HEAD * (hd + 1)] * SB_SCALE).astype(cdt) for hd in range(SB_HEADS)]
        kvs = [[ref[pl.ds(hd, w, stride=SB_HEADS), :].astype(cdt) for hd in range(SB_HEADS)] for ref in pages]
        z = jnp.concatenate(
            [jnp.concatenate([_dot_nt(qs[hd], kv[hd]) for hd in range(SB_HEADS)], axis=0) for kv in kvs],
            axis=1)
        log_keep, log_break = _sb_terms(z)
        lane = lax.broadcasted_iota(jnp.int32, z.shape, 1)
        mask = jnp.bitwise_and(lane, 1) == 0
        if causal:
            tok = jnp.bitwise_and(lax.broadcasted_iota(jnp.int32, z.shape, 0), T_PAD - 1)
            mask = mask & (lax.shift_right_logical(lane, 1) < tok)
        log_keep = jnp.where(mask, log_keep, 0.0)
        later = []
        for p in range(len(pages)):
            lk = log_keep[:, w * p:w * (p + 1)]
            later.append(later_of(lk) + keep)
            keep = keep + jnp.sum(lk, axis=-1, keepdims=True)
        a = jnp.where(mask, jnp.exp(log_break + jnp.concatenate(later, axis=1)), 0.0)
        a = shift_of(a)
        outs = []
        for hd in range(SB_HEADS):
            rows = slice(T_PAD * hd, T_PAD * (hd + 1))
            o = None
            for p, kv in enumerate(kvs):
                d = _dot(a[rows, w * p:w * (p + 1)].astype(cdt), kv[hd])
                o = d if o is None else o + d
            outs.append(o)
        return keep, acc + jnp.concatenate(outs, axis=0)

    @pl.when(s == 0)
    def _():
        exact = functools.partial(jnp.dot, precision=lax.Precision.HIGHEST, preferred_element_type=F32)
        keep, acc = process([new_ref], T_PAD, True, F32, jnp.zeros((n_rows, 1), F32),
                            jnp.zeros((n_rows, SB_HEAD), F32),
                            lambda lk: exact(lk, un_ref[...]), lambda a: exact(a, shn_ref[...]))
        keep_sc[...] = keep
        acc_sc[...] = acc

    @pl.when(s > 0)
    def _():
        keep, acc = process(page_refs, PAGE_SIZE, False, BF16, keep_sc[...], acc_sc[...],
                            lambda lk: _dot01_right(lk, u_ref[...]),
                            lambda a: pltpu.roll(a, 1, 1))
        keep_sc[...] = keep
        acc_sc[...] = acc

    @pl.when(s == pl.num_programs(1) - 1)
    def _():
        for hd in range(SB_HEADS):
            o_ref[:, SB_HEAD * hd:SB_HEAD * (hd + 1)] = acc_sc[T_PAD * hd:T_PAD * (hd + 1), :]


def sb_decode(page_table, new_page, cache_rows, layer, q, n_slots):
    db, n_pages = page_table.shape
    n_rows = T_PAD * SB_HEADS
    hw = SB_HEADS * SB_HEAD
    page_rows = cache_rows.shape[2]
    assert n_pages % n_slots == 0
    steps = n_pages // n_slots

    def page_spec(slot):
        def imap(b, s, pt):
            idx = n_pages - 1 - (jnp.maximum(s - 1, 0) * n_slots + slot)
            return (layer, pt[b * n_pages + idx], 0, 0)
        return pl.BlockSpec((None, None, page_rows, SB_HEAD), imap)

    grid_spec = pltpu.PrefetchScalarGridSpec(
        num_scalar_prefetch=1,
        grid=(db, steps + 1),
        in_specs=[pl.BlockSpec((None,) + new_page.shape[1:], lambda b, s, pt: (b, 0, 0))]
        + [page_spec(i) for i in range(n_slots)]
        + [pl.BlockSpec((None, T_PAD, hw), lambda b, s, pt: (b, 0, 0))]
        + [pl.BlockSpec((2 * n, 2 * n), lambda b, s, pt: (0, 0)) for n in (PAGE_SIZE, T_PAD, T_PAD)],
        out_specs=pl.BlockSpec((None, T_PAD, hw), lambda b, s, pt: (b, 0, 0)),
        scratch_shapes=[pltpu.VMEM((n_rows, 1), F32), pltpu.VMEM((n_rows, SB_HEAD), F32)],
    )
    return pl.pallas_call(
        functools.partial(_sb_dec_kernel, n_slots=n_slots),
        out_shape=jax.ShapeDtypeStruct((db, T_PAD, hw), F32),
        grid_spec=grid_spec,
        compiler_params=_cparams(("arbitrary", "arbitrary")),
        name="sb_decode",
    )(page_table.reshape(-1), new_page, *([cache_rows] * n_slots), q,
      _interleaved_matrices(PAGE_SIZE, BF16)[0], *_interleaved_matrices(T_PAD, F32))


def _interleaved_matrices(n_keys, dtype):
    r = lax.broadcasted_iota(jnp.int32, (2 * n_keys, 2 * n_keys), 0)
    c = lax.broadcasted_iota(jnp.int32, (2 * n_keys, 2 * n_keys), 1)
    return ((r // 2) > (c // 2)).astype(dtype), (c == r + 1).astype(dtype)


TM_LINEAR = 512
TM_FFN = 1024
TF_FFN = 256
TQ_MLA, TK_MLA = 512, 512
TQ_SB, TK_SB = 512, 256
TS_HGRN = 256
MLA_PAGES_PER_STEP = 8
SB_PAGES_PER_STEP = 8


def _pad_tokens(x, t_pad):
    return jnp.pad(x, ((0, 0), (0, t_pad - x.shape[1])) + ((0, 0),) * (x.ndim - 2))


def kernel(x_prompt, x_sample, c_prompt, c_sample, page_table, cache_mla, cache_sb_kv, state_hgrn,
           state_ffn_conv, w_mod, b_mod, norm_g, w_mla_a, g_mla_qa, g_mla_kva, w_mla_uq, w_mla_ukv,
           g_mla_q, g_mla_k, w_mla_o, w_sb_qkv, w_sb_o, w_hg_in, hg_lb_logits, g_hg_o, w_hg_o,
           w_ffn_up, ffn_conv_w, ffn_conv_b, w_ffn_down):
    b, s_len, d = x_prompt.shape
    db, t, _ = x_sample.shape
    depth = w_mod.shape[0]
    n_pages = page_table.shape[1]
    past = n_pages * PAGE_SIZE
    dff = w_ffn_down.shape[1]
    assert t <= T_PAD and t >= CONV_W - 1

    mod = ada_mod(jnp.concatenate([c_prompt, c_sample], axis=0), w_mod, b_mod)
    mod = mod.reshape(depth, b + db, N_MOD, d)
    cos_p, sin_p = _rope_tables(jnp.arange(s_len))
    cos_s, sin_s = _rope_tables(jnp.tile(past + jnp.arange(T_PAD), db))
    sb_cache = cache_sb_kv.reshape(cache_sb_kv.shape[:2] + (PAGE_SIZE * 2 * SB_HEADS, SB_HEAD))
    mla_cache_t = jnp.swapaxes(cache_mla, 2, 3)

    xp = x_prompt
    xs = _pad_tokens(x_sample, T_PAD).reshape(1, db * T_PAD, d)
    mla_p, mla_s, sb_p, sb_s, hg_p, hg_s, ffn_p, ffn_s = [], [], [], [], [], [], [], []
    for i in range(depth):
        mp = [mod[i, :b, j].reshape(b, 1, d) for j in range(N_MOD)]
        ms = [jnp.repeat(mod[i, b:, j], T_PAD, axis=0).reshape(1, db * T_PAD, d) for j in range(N_MOD)]
        kind, j = i % N_MIXERS, i // N_MIXERS
        if kind == 0:
            wp = _mla_weights(w_mla_a[j], g_mla_qa[j], g_mla_kva[j], w_mla_uq[j], w_mla_ukv[j],
                              g_mla_q[j], g_mla_k[j])
            rows_p, qp, kp, vp = mla_proj(xp, norm_g[i, 0], mp[0], mp[1], cos_p, sin_p, wp, BF16, TM_LINEAR)
            op = flash_mla(qp, kp, vp, TQ_MLA, TK_MLA)
            rows_s, qs, _, _ = mla_proj(xs, norm_g[i, 0], ms[0], ms[1], cos_s, sin_s, wp, F32, TM_LINEAR)
            qa, qpe = mla_absorb(qs[0], wp["wk"], wp["gk"])
            qa = qa.reshape(db, T_PAD * MLA_HEADS, KV_LORA)
            qpe = qpe.reshape(db, T_PAD * MLA_HEADS, LANES)[:, :, :QK_ROPE]
            rows_s = rows_s.reshape(db, T_PAD, -1)
            new_page = jnp.swapaxes(_pad_tokens(rows_s, PAGE_SIZE), 1, 2)
            o_lat = mla_decode(page_table, new_page, mla_cache_t, j, qa, qpe, wp["wk_t"], MLA_PAGES_PER_STEP)
            os_ = mla_vup(o_lat.reshape(db * T_PAD, MLA_HEADS * KV_LORA), wp["wv"])
            os_ = os_.reshape(1, db * T_PAD, -1)
            w_o = w_mla_o[j]
            mla_p.append(rows_p)
            mla_s.append(rows_s[:, :t])
        elif kind == 1:
            wq = w_sb_qkv[j].astype(BF16)
            hw = SB_HEADS * SB_HEAD
            q_p, kv_p = norm_mod_linear(xp, norm_g[i, 0], mp[0], mp[1], [wq[:, :hw], wq[:, hw:]], TM_LINEAR)
            op = sb_attention(q_p, kv_p, TQ_SB, TK_SB)
            q_s, kv_s = norm_mod_linear(xs, norm_g[i, 0], ms[0], ms[1], [wq[:, :hw], wq[:, hw:]], TM_LINEAR)
            kv_s = kv_s.reshape(db, T_PAD, 2 * hw)
            new_page = kv_s.reshape(db, T_PAD * 2 * SB_HEADS, SB_HEAD)
            os_ = sb_decode(page_table, new_page, sb_cache, j, q_s.reshape(db, T_PAD, hw), SB_PAGES_PER_STEP)
            os_ = os_.reshape(1, db * T_PAD, hw)
            w_o = w_sb_o[j]
            sb_p.append(kv_p.reshape(b, s_len, 2, SB_HEADS, SB_HEAD))
            sb_s.append(kv_s[:, :t].reshape(db, t, 2, SB_HEADS, SB_HEAD))
        else:
            w_in = w_hg_in[j].astype(BF16)
            (zp,) = norm_mod_linear(xp, norm_g[i, 0], mp[0], mp[1], [w_in], TM_LINEAR)
            zero_state = jnp.zeros((b, HG_HEADS, HG_DK, HG_DV), F32)
            op, st_p = hgrn_scan(zp, hg_lb_logits, zero_state, g_hg_o[j], i, HG_CHUNK, s_len, TS_HGRN, BF16)
            (zs,) = norm_mod_linear(xs, norm_g[i, 0], ms[0], ms[1], [w_in], TM_LINEAR)
            os_, st_s = hgrn_scan(zs.reshape(db, T_PAD, -1), hg_lb_logits, state_hgrn[j], g_hg_o[j], i,
                                  T_PAD, t, T_PAD, F32)
            os_ = os_.reshape(1, db * T_PAD, -1)
            w_o = w_hg_o[j]
            hg_p.append(st_p)
            hg_s.append(st_s)
        w_o = w_o.astype(BF16)
        xp = proj_residual(op, w_o, xp, mp[2], TM_LINEAR)
        xs = proj_residual(os_, w_o, xs, ms[2], TM_LINEAR)

        w_up = w_ffn_up[i].astype(BF16)
        w_down = w_ffn_down[i].astype(BF16)
        xp, tail_p = conv_ffn(xp, norm_g[i, 1], mp[3], mp[4], mp[5], w_up, ffn_conv_w[i], ffn_conv_b[i],
                              w_down, None, TM_FFN, TF_FFN)
        st = state_ffn_conv[i]
        z = jnp.zeros((db, T_PAD - 2, dff), F32)
        past2 = jnp.concatenate([st, z], axis=1)
        past1 = jnp.concatenate([st[:, 1:], z, z[:, :1]], axis=1)
        past_rows = jnp.stack([past2, past1]).reshape(CONV_W - 1, db * T_PAD, dff)
        xs, a_s = conv_ffn(xs, norm_g[i, 1], ms[3], ms[4], ms[5], w_up, ffn_conv_w[i], ffn_conv_b[i],
                           w_down, past_rows, TM_FFN, TF_FFN)
        ffn_p.append(tail_p[:, -1, SUBLANES - (CONV_W - 1):])
        ffn_s.append(a_s.reshape(db, T_PAD, dff)[:, t - (CONV_W - 1):t])
    ys = xs.reshape(db, T_PAD, d)[:, :t]
    return (xp, ys, jnp.stack(mla_p), jnp.stack(mla_s), jnp.stack(sb_p), jnp.stack(sb_s),
            jnp.stack(hg_p), jnp.stack(hg_s), jnp.stack(ffn_p), jnp.stack(ffn_s))
```

```python
import functools

import jax
import jax.numpy as jnp
import numpy as np
from jax import lax
from jax.experimental import pallas as pl
from jax.experimental.pallas import tpu as pltpu

F32 = jnp.float32
BF16 = jnp.bfloat16

N_MIXERS = 3
MLA_HEADS = 8
QK_NOPE = 128
QK_ROPE = 64
V_HEAD = 128
Q_LORA = 256
KV_LORA = 256
QK_HEAD = QK_NOPE + QK_ROPE
MLA_SCALE = QK_HEAD ** -0.5
ROPE_BASE = 10000.0
SB_HEADS = 8
SB_HEAD = 128
SB_SCALE = SB_HEAD ** -0.5
HG_HEADS = 8
HG_DK = 128
HG_DV = 128
HG_CHUNK = 32
CONV_W = 3
PAGE_SIZE = 128
EPS = 1e-6
N_MOD = 6

LANES = 128
SUBLANES = 8
T_PAD = SUBLANES
NEG_BIG = -1e30
HEAD_SHIFT = 3
LANE_SHIFT = 7
VMEM_LIMIT = 48 * 1024 * 1024


def _cparams(sem):
    return pltpu.CompilerParams(dimension_semantics=sem, vmem_limit_bytes=VMEM_LIMIT)


def _dot(a, b):
    return jnp.dot(a, b, preferred_element_type=F32)


def _dot_nt(a, b):
    return lax.dot_general(a, b, (((1,), (1,)), ((), ())), preferred_element_type=F32)


def _dot_tn(a, b):
    return lax.dot_general(a, b, (((0,), (0,)), ((), ())), preferred_element_type=F32)


def _split_bf16(x, parts):
    out = []
    r = x
    for _ in range(parts):
        p = r.astype(BF16)
        out.append(p)
        r = r - p.astype(F32)
    return out


def _dot01(m01, x, parts=3):
    acc = None
    for p in _split_bf16(x, parts):
        d = _dot(m01, p)
        acc = d if acc is None else acc + d
    return acc


def _dot01_right(x, m01, parts=2):
    acc = None
    for p in _split_bf16(x, parts):
        d = _dot(p, m01)
        acc = d if acc is None else acc + d
    return acc


def _rms(x, g):
    ms = jnp.mean(x * x, axis=-1, keepdims=True)
    return x * lax.rsqrt(ms + EPS) * g


def _norm_mod(x, g, shift, scale):
    return _rms(x, g) * (1.0 + scale) + shift


def _silu(x):
    return x * jax.nn.sigmoid(x)


def _mod_kernel(c_ref, w_ref, b_ref, o_ref):
    s = _silu(c_ref[...])
    o_ref[...] = _dot(s.astype(BF16), w_ref[...].astype(BF16)) + b_ref[...]


def ada_mod(c_all, w_mod, b_mod):
    depth, d, n = w_mod.shape
    nc = c_all.shape[0]
    tn = n // 4
    return pl.pallas_call(
        _mod_kernel,
        out_shape=jax.ShapeDtypeStruct((depth, nc, n), F32),
        grid=(depth, n // tn),
        in_specs=[
            pl.BlockSpec((nc, d), lambda l, j: (0, 0)),
            pl.BlockSpec((None, d, tn), lambda l, j: (l, 0, j)),
            pl.BlockSpec((None, 1, tn), lambda l, j: (l, 0, j)),
        ],
        out_specs=pl.BlockSpec((None, nc, tn), lambda l, j: (l, 0, j)),
        compiler_params=_cparams(("arbitrary", "arbitrary")),
        name="ada_mod",
    )(c_all, w_mod, b_mod.reshape(depth, 1, n))


def _mod_spec(mod, tm):
    r = mod.shape[1]
    if r == 1:
        return pl.BlockSpec((None, 1, mod.shape[2]), lambda b, i, *_: (b, 0, 0))
    return pl.BlockSpec((None, tm, mod.shape[2]), lambda b, i, *_: (b, i, 0))


def _nml_kernel(x_ref, g_ref, sh_ref, sc_ref, *rest, n_w):
    h = _norm_mod(x_ref[...], g_ref[...], sh_ref[...], sc_ref[...]).astype(BF16)
    for w_ref, o_ref in zip(rest[:n_w], rest[n_w:]):
        o_ref[...] = _dot(h, w_ref[...]).astype(o_ref.dtype)


def norm_mod_linear(x, g, shift, scale, ws, tm):
    nb, s, d = x.shape
    tm = min(tm, s)
    in_specs = [
        pl.BlockSpec((None, tm, d), lambda b, i: (b, i, 0)),
        pl.BlockSpec((1, d), lambda b, i: (0, 0)),
        _mod_spec(shift, tm),
        _mod_spec(scale, tm),
    ]
    out_shape, out_specs = [], []
    for w in ws:
        in_specs.append(pl.BlockSpec(w.shape, lambda b, i: (0, 0)))
        out_shape.append(jax.ShapeDtypeStruct((nb, s, w.shape[1]), F32))
        out_specs.append(pl.BlockSpec((None, tm, w.shape[1]), lambda b, i: (b, i, 0)))
    return pl.pallas_call(
        functools.partial(_nml_kernel, n_w=len(ws)),
        out_shape=out_shape,
        grid=(nb, s // tm),
        in_specs=in_specs,
        out_specs=out_specs,
        compiler_params=_cparams(("arbitrary", "arbitrary")),
        name="norm_mod_linear",
    )(x, g.reshape(1, d), shift, scale, *ws)


def _proj_res_kernel(o_ref, w_ref, x_ref, gate_ref, out_ref):
    y = _dot(o_ref[...].astype(BF16), w_ref[...])
    out_ref[...] = x_ref[...] + gate_ref[...] * y


def proj_residual(o, w, x, gate, tm):
    nb, s, d = x.shape
    din = o.shape[2]
    tm = min(tm, s)
    return pl.pallas_call(
        _proj_res_kernel,
        out_shape=jax.ShapeDtypeStruct((nb, s, d), F32),
        grid=(nb, s // tm),
        in_specs=[
            pl.BlockSpec((None, tm, din), lambda b, i: (b, i, 0)),
            pl.BlockSpec((din, d), lambda b, i: (0, 0)),
            pl.BlockSpec((None, tm, d), lambda b, i: (b, i, 0)),
            _mod_spec(gate, tm),
        ],
        out_specs=pl.BlockSpec((None, tm, d), lambda b, i: (b, i, 0)),
        compiler_params=_cparams(("arbitrary", "arbitrary")),
        name="proj_residual",
    )(o, w, x, gate)


def _mla_proj_kernel(x_ref, g_ref, sh_ref, sc_ref, cos_ref, sin_ref, wa_ref, gqa_ref, gkva_ref,
                     wqm_ref, wqs_ref, wkv_ref, gq_ref, gk_ref,
                     rows_ref, q_ref, k_ref, v_ref):
    h = _norm_mod(x_ref[...], g_ref[...], sh_ref[...], sc_ref[...]).astype(BF16)
    a = _dot(h, wa_ref[...])
    cos = cos_ref[...]
    sin = sin_ref[...]
    qn = _rms(a[:, :Q_LORA], gqa_ref[...]).astype(BF16)
    cn = _rms(a[:, Q_LORA:Q_LORA + KV_LORA], gkva_ref[...])
    o = Q_LORA + KV_LORA
    kpe = a[:, o:o + LANES] * cos + a[:, o + LANES:o + 2 * LANES] * sin
    rows_ref[...] = jnp.concatenate([cn, kpe[:, :QK_ROPE]], axis=-1)
    qm = _dot(qn, wqm_ref[...])
    qs = _dot(qn, wqs_ref[...])
    kv = _dot(cn.astype(BF16), wkv_ref[...])
    gq = gq_ref[...]
    gk = gk_ref[...]
    kpe_ss = jnp.sum(kpe * kpe, axis=-1, keepdims=True)
    inv = 1.0 / QK_HEAD
    for hd in range(MLA_HEADS):
        c0 = 2 * LANES * hd
        nope = qm[:, c0:c0 + LANES]
        pe = qm[:, c0 + LANES:c0 + 2 * LANES] * cos + qs[:, LANES * hd:LANES * (hd + 1)] * sin
        ss = jnp.sum(nope * nope, axis=-1, keepdims=True) + jnp.sum(pe * pe, axis=-1, keepdims=True)
        r = lax.rsqrt(ss * inv + EPS) * MLA_SCALE
        q_ref[:, c0:c0 + LANES] = (nope * r * gq[:, :LANES]).astype(q_ref.dtype)
        q_ref[:, c0 + LANES:c0 + 2 * LANES] = (pe * r * gq[:, LANES:]).astype(q_ref.dtype)
        kn = kv[:, c0:c0 + LANES]
        rk = lax.rsqrt((jnp.sum(kn * kn, axis=-1, keepdims=True) + kpe_ss) * inv + EPS)
        k_ref[:, c0:c0 + LANES] = (kn * rk * gk[:, :LANES]).astype(k_ref.dtype)
        k_ref[:, c0 + LANES:c0 + 2 * LANES] = (kpe * rk * gk[:, LANES:]).astype(k_ref.dtype)
        v_ref[:, LANES * hd:LANES * (hd + 1)] = kv[:, c0 + LANES:c0 + 2 * LANES].astype(v_ref.dtype)


def mla_proj(x, g, shift, scale, cos, sin, wp, q_dtype, tm):
    nb, s, d = x.shape
    tm = min(tm, s)
    hw = 2 * LANES * MLA_HEADS
    full = lambda arr: pl.BlockSpec(arr.shape, lambda b, i: (0, 0))
    return pl.pallas_call(
        _mla_proj_kernel,
        out_shape=[
            jax.ShapeDtypeStruct((nb, s, KV_LORA + QK_ROPE), F32),
            jax.ShapeDtypeStruct((nb, s, hw), q_dtype),
            jax.ShapeDtypeStruct((nb, s, hw), BF16),
            jax.ShapeDtypeStruct((nb, s, LANES * MLA_HEADS), BF16),
        ],
        grid=(nb, s // tm),
        in_specs=[
            pl.BlockSpec((None, tm, d), lambda b, i: (b, i, 0)),
            pl.BlockSpec((1, d), lambda b, i: (0, 0)),
            _mod_spec(shift, tm),
            _mod_spec(scale, tm),
            pl.BlockSpec((tm, LANES), lambda b, i: (i, 0)),
            pl.BlockSpec((tm, LANES), lambda b, i: (i, 0)),
            full(wp["wa"]), full(wp["gqa"]), full(wp["gkva"]), full(wp["wqm"]), full(wp["wqs"]),
            full(wp["wkv"]), full(wp["gq"]), full(wp["gk"]),
        ],
        out_specs=[
            pl.BlockSpec((None, tm, KV_LORA + QK_ROPE), lambda b, i: (b, i, 0)),
            pl.BlockSpec((None, tm, hw), lambda b, i: (b, i, 0)),
            pl.BlockSpec((None, tm, hw), lambda b, i: (b, i, 0)),
            pl.BlockSpec((None, tm, LANES * MLA_HEADS), lambda b, i: (b, i, 0)),
        ],
        compiler_params=_cparams(("arbitrary", "arbitrary")),
        name="mla_proj",
    )(x, g.reshape(1, d), shift, scale, cos, sin, wp["wa"], wp["gqa"], wp["gkva"], wp["wqm"], wp["wqs"],
      wp["wkv"], wp["gq"], wp["gk"])


def _mla_weights(w_a, g_qa, g_kva, w_uq, w_ukv, g_q, g_k):
    d = w_a.shape[0]
    half = QK_ROPE // 2
    zpad = jnp.zeros((d, LANES - QK_ROPE), F32)
    w_pe = w_a[:, Q_LORA + KV_LORA:]
    w_pe_sw = jnp.concatenate([w_pe[:, half:], w_pe[:, :half]], axis=1)
    wa = jnp.concatenate([w_a[:, :Q_LORA + KV_LORA], w_pe, zpad, w_pe_sw, zpad], axis=1).astype(BF16)
    wq = w_uq.reshape(Q_LORA, MLA_HEADS, QK_HEAD)
    wq_pe = wq[:, :, QK_NOPE:]
    z = jnp.zeros((Q_LORA, MLA_HEADS, LANES - QK_ROPE), F32)
    wqm = jnp.concatenate([wq, z], axis=2).reshape(Q_LORA, MLA_HEADS * 2 * LANES).astype(BF16)
    wq_sw = jnp.concatenate([wq_pe[:, :, half:], wq_pe[:, :, :half], z], axis=2)
    wqs = wq_sw.reshape(Q_LORA, MLA_HEADS * LANES).astype(BF16)
    gpad = jnp.zeros((2 * LANES - QK_HEAD,), F32)
    wkv3 = w_ukv.reshape(KV_LORA, MLA_HEADS, QK_NOPE + V_HEAD)
    return {
        "wa": wa,
        "gqa": g_qa.reshape(1, Q_LORA),
        "gkva": g_kva.reshape(1, KV_LORA),
        "wqm": wqm,
        "wqs": wqs,
        "wkv": w_ukv.astype(BF16),
        "gq": jnp.concatenate([g_q, gpad]).reshape(1, 2 * LANES),
        "gk": jnp.concatenate([g_k, gpad]).reshape(1, 2 * LANES),
        "wk_t": wkv3[:, :, :QK_NOPE].reshape(KV_LORA, MLA_HEADS * QK_NOPE).T.astype(BF16),
        "wk": wkv3[:, :, :QK_NOPE].reshape(KV_LORA, MLA_HEADS * QK_NOPE).astype(BF16),
        "wv": wkv3[:, :, QK_NOPE:].reshape(KV_LORA, MLA_HEADS * V_HEAD).astype(BF16),
    }


def _rope_tables(pos):
    half = QK_ROPE // 2
    inv = ROPE_BASE ** (-jnp.arange(half, dtype=F32) / half)
    ang = pos.astype(F32)[:, None] * inv[None, :]
    cos, sin = jnp.cos(ang), jnp.sin(ang)
    z = jnp.zeros((pos.shape[0], LANES - QK_ROPE), F32)
    return jnp.concatenate([cos, cos, z], axis=1), jnp.concatenate([-sin, sin, z], axis=1)


def _pairs(n_q, tq, tk, newest_first):
    qi, ki = [], []
    for a in range(n_q):
        last = ((a + 1) * tq - 1) // tk
        ks = range(last, -1, -1) if newest_first else range(last + 1)
        for b in ks:
            qi.append(a)
            ki.append(b)
    return jnp.asarray(qi, jnp.int32), jnp.asarray(ki, jnp.int32)


def _flash_kernel(qi_ref, ki_ref, q_ref, k_ref, v_ref, o_ref, m_sc, l_sc, acc_sc, *, tq, tk, hps, tr):
    p = pl.program_id(2)
    qi = qi_ref[p]
    ki = ki_ref[p]

    @pl.when(ki == 0)
    def _():
        m_sc[...] = jnp.full(m_sc.shape, NEG_BIG, F32)
        l_sc[...] = jnp.zeros(l_sc.shape, F32)
        acc_sc[...] = jnp.zeros(acc_sc.shape, F32)

    def update(masked):
        for hh in range(hps):
            qk = slice(2 * LANES * hh, 2 * LANES * (hh + 1))
            vc = slice(LANES * hh, LANES * (hh + 1))
            for r0 in range(0, tq, tr):
                rows = slice(r0, r0 + tr)
                s = _dot_nt(q_ref[rows, qk], k_ref[:, qk])
                if masked:
                    row = qi * tq + r0 + lax.broadcasted_iota(jnp.int32, (tr, tk), 0)
                    col = ki * tk + lax.broadcasted_iota(jnp.int32, (tr, tk), 1)
                    s = jnp.where(row >= col, s, NEG_BIG)
                m_prev = m_sc[hh, rows]
                m_new = jnp.maximum(m_prev, jnp.max(s, axis=-1, keepdims=True))
                alpha = jnp.exp(m_prev - m_new)
                pr = jnp.exp(s - m_new)
                l_sc[hh, rows] = alpha * l_sc[hh, rows] + jnp.sum(pr, axis=-1, keepdims=True)
                acc_sc[rows, vc] = alpha * acc_sc[rows, vc] + _dot(pr.astype(BF16), v_ref[:, vc])
                m_sc[hh, rows] = m_new

    on_diagonal = (ki + 1) * tk > qi * tq

    @pl.when(on_diagonal)
    def _():
        update(True)

    @pl.when(jnp.logical_not(on_diagonal))
    def _():
        update(False)

    @pl.when(ki == ((qi + 1) * tq - 1) // tk)
    def _():
        for hh in range(hps):
            vc = slice(LANES * hh, LANES * (hh + 1))
            o_ref[:, vc] = (acc_sc[:, vc] / l_sc[hh]).astype(o_ref.dtype)


def flash_mla(q, k, v, tq, tk, hps, tr):
    nb, s, _ = q.shape
    tq, tk = min(tq, s), min(tk, s)
    tr = min(tr, tq)
    assert MLA_HEADS % hps == 0 and tq % tr == 0
    qi, ki = _pairs(s // tq, tq, tk, newest_first=False)
    grid_spec = pltpu.PrefetchScalarGridSpec(
        num_scalar_prefetch=2,
        grid=(nb, MLA_HEADS // hps, qi.shape[0]),
        in_specs=[
            pl.BlockSpec((None, tq, hps * 2 * LANES), lambda b, h, p, qi, ki: (b, qi[p], h)),
            pl.BlockSpec((None, tk, hps * 2 * LANES), lambda b, h, p, qi, ki: (b, ki[p], h)),
            pl.BlockSpec((None, tk, hps * LANES), lambda b, h, p, qi, ki: (b, ki[p], h)),
        ],
        out_specs=pl.BlockSpec((None, tq, hps * LANES), lambda b, h, p, qi, ki: (b, qi[p], h)),
        scratch_shapes=[pltpu.VMEM((hps, tq, 1), F32), pltpu.VMEM((hps, tq, 1), F32),
                        pltpu.VMEM((tq, hps * LANES), F32)],
    )
    return pl.pallas_call(
        functools.partial(_flash_kernel, tq=tq, tk=tk, hps=hps, tr=tr),
        out_shape=jax.ShapeDtypeStruct((nb, s, MLA_HEADS * LANES), BF16),
        grid_spec=grid_spec,
        compiler_params=_cparams(("arbitrary", "arbitrary", "arbitrary")),
        name="flash_mla",
    )(qi, ki, q, k, v)


def _sb_terms(z):
    tail = jnp.log(1.0 + jnp.exp(-jnp.abs(z)))
    return -(jnp.maximum(z, 0.0) + tail), jnp.minimum(z, 0.0) - tail


def _sb_kernel(qi_ref, ki_ref, q_ref, k_ref, v_ref, u_ref, o_ref, keep_sc, acc_sc, *, tq, tk, hps, tr):
    p = pl.program_id(2)
    qi = qi_ref[p]
    ki = ki_ref[p]

    @pl.when(ki == ((qi + 1) * tq - 1) // tk)
    def _():
        keep_sc[...] = jnp.zeros(keep_sc.shape, F32)
        acc_sc[...] = jnp.zeros(acc_sc.shape, F32)

    def update(masked):
        for hh in range(hps):
            hc = slice(SB_HEAD * hh, SB_HEAD * (hh + 1))
            k = k_ref[:, hc].astype(BF16)
            v = v_ref[:, hc].astype(BF16)
            for r0 in range(0, tq, tr):
                rows = slice(r0, r0 + tr)
                q = (q_ref[rows, hc] * SB_SCALE).astype(BF16)
                log_keep, log_break = _sb_terms(_dot_nt(q, k))
                if masked:
                    row = qi * tq + r0 + lax.broadcasted_iota(jnp.int32, (tr, tk), 0)
                    col = ki * tk + lax.broadcasted_iota(jnp.int32, (tr, tk), 1)
                    mask = row > col
                    log_keep = jnp.where(mask, log_keep, 0.0)
                a = jnp.exp(log_break + _dot01_right(log_keep, u_ref[...]) + keep_sc[hh, rows])
                if masked:
                    a = jnp.where(mask, a, 0.0)
                acc_sc[rows, hc] += _dot(a.astype(BF16), v)
                keep_sc[hh, rows] += jnp.sum(log_keep, axis=-1, keepdims=True)

    on_diagonal = (ki + 1) * tk > qi * tq

    @pl.when(on_diagonal)
    def _():
        update(True)

    @pl.when(jnp.logical_not(on_diagonal))
    def _():
        update(False)

    @pl.when(ki == 0)
    def _():
        o_ref[...] = acc_sc[...].astype(o_ref.dtype)


def _later_matrix(n):
    return (lax.broadcasted_iota(jnp.int32, (n, n), 0) > lax.broadcasted_iota(jnp.int32, (n, n), 1)).astype(BF16)


def sb_attention(q, kv, tq, tk, hps, tr):
    nb, s, _ = q.shape
    tq, tk = min(tq, s), min(tk, s)
    tr = min(tr, tq)
    assert SB_HEADS % hps == 0 and tq % tr == 0
    n_groups = SB_HEADS // hps
    qi, ki = _pairs(s // tq, tq, tk, newest_first=True)
    grid_spec = pltpu.PrefetchScalarGridSpec(
        num_scalar_prefetch=2,
        grid=(nb, n_groups, qi.shape[0]),
        in_specs=[
            pl.BlockSpec((None, tq, hps * SB_HEAD), lambda b, h, p, qi, ki: (b, qi[p], h)),
            pl.BlockSpec((None, tk, hps * SB_HEAD), lambda b, h, p, qi, ki: (b, ki[p], h)),
            pl.BlockSpec((None, tk, hps * SB_HEAD), lambda b, h, p, qi, ki: (b, ki[p], n_groups + h)),
            pl.BlockSpec((tk, tk), lambda b, h, p, qi, ki: (0, 0)),
        ],
        out_specs=pl.BlockSpec((None, tq, hps * SB_HEAD), lambda b, h, p, qi, ki: (b, qi[p], h)),
        scratch_shapes=[pltpu.VMEM((hps, tq, 1), F32), pltpu.VMEM((tq, hps * SB_HEAD), F32)],
    )
    return pl.pallas_call(
        functools.partial(_sb_kernel, tq=tq, tk=tk, hps=hps, tr=tr),
        out_shape=jax.ShapeDtypeStruct((nb, s, SB_HEADS * SB_HEAD), BF16),
        grid_spec=grid_spec,
        compiler_params=_cparams(("arbitrary", "arbitrary", "arbitrary")),
        name="sb_attention",
    )(qi, ki, q, kv, kv, _later_matrix(tk))


def _hgrn_kernel(x_ref, lbl_ref, st0_ref, go_ref, mc_ref, ml_ref, o_ref, stf_ref,
                 st_sc, qd_sc, kd_sc, kl_sc, el_sc, *, layer, chunk, t_valid, ts):
    s = pl.program_id(1)
    hk = HG_HEADS * HG_DK
    cdt = BF16 if chunk % 16 == 0 else F32

    @pl.when(s == 0)
    def _():
        for hd in range(HG_HEADS):
            st_sc[hd] = st0_ref[hd].T

    logits = lbl_ref[...]
    e = jnp.exp(logits - jnp.max(logits, axis=0, keepdims=True))
    sm = e / jnp.sum(e, axis=0, keepdims=True)
    lb = jnp.sum(sm[1:layer + 1], axis=0, keepdims=True)
    q = x_ref[:, :hk]
    forget = lb + (1.0 - lb) * jax.nn.sigmoid(x_ref[:, hk:2 * hk])
    k = 1.0 - forget
    log_f = jnp.log(forget)
    if t_valid < ts:
        valid = lax.broadcasted_iota(jnp.int32, (ts, hk), 0) < t_valid
        log_f = jnp.where(valid, log_f, 0.0)
        k = jnp.where(valid, k, 0.0)
    if cdt == BF16:
        cum = _dot01(mc_ref[...], log_f)
        last = _dot01(ml_ref[...], log_f)
    else:
        cum = jnp.dot(mc_ref[...], log_f, precision=lax.Precision.HIGHEST, preferred_element_type=F32)
        last = jnp.dot(ml_ref[...], log_f, precision=lax.Precision.HIGHEST, preferred_element_type=F32)
    qd_sc[...] = q * jnp.exp(cum)
    kd_sc[...] = k * jnp.exp(-cum)
    kl_sc[...] = k * jnp.exp(last - cum)
    el_sc[...] = jnp.exp(last)
    go = go_ref[...]
    causal = (lax.broadcasted_iota(jnp.int32, (chunk, chunk), 0)
              >= lax.broadcasted_iota(jnp.int32, (chunk, chunk), 1))

    hv = HG_HEADS * HG_DV
    for hd in range(HG_HEADS):
        cols = slice(HG_DK * hd, HG_DK * (hd + 1))
        vcols = slice(2 * hk + HG_DV * hd, 2 * hk + HG_DV * (hd + 1))
        gcols = slice(2 * hk + hv + HG_DV * hd, 2 * hk + hv + HG_DV * (hd + 1))
        st = st_sc[hd]
        for c in range(ts // chunk):
            rows = slice(c * chunk, (c + 1) * chunk)
            qd = qd_sc[rows, cols].astype(cdt)
            kd = kd_sc[rows, cols].astype(cdt)
            kl = kl_sc[rows, cols].astype(cdt)
            v = x_ref[rows, vcols].astype(cdt)
            el = el_sc[c * chunk:c * chunk + 1, cols]
            att = jnp.where(causal, _dot_nt(qd, kd), 0.0)
            o = _dot_nt(qd, st.astype(cdt)) + _dot(att.astype(cdt), v)
            st = st * el + _dot_tn(v, kl)
            on = _rms(o, go) * _silu(x_ref[rows, gcols])
            o_ref[rows, HG_DV * hd:HG_DV * (hd + 1)] = on.astype(o_ref.dtype)
        st_sc[hd] = st

    @pl.when(s == pl.num_programs(1) - 1)
    def _():
        for hd in range(HG_HEADS):
            stf_ref[hd] = st_sc[hd].T


def _chunk_matrices(ts, chunk):
    r = lax.broadcasted_iota(jnp.int32, (ts, ts), 0)
    c = lax.broadcasted_iota(jnp.int32, (ts, ts), 1)
    same = (r // chunk) == (c // chunk)
    dtype = BF16 if chunk % 16 == 0 else F32
    return (same & (c <= r)).astype(dtype), same.astype(dtype)


def hgrn_scan(x, lb_logits, state0, g_o, layer, chunk, t_valid, ts, out_dtype):
    nb, s, _ = x.shape
    ts = min(ts, s)
    hk, hv = HG_HEADS * HG_DK, HG_HEADS * HG_DV
    mc, ml = _chunk_matrices(ts, chunk)
    return pl.pallas_call(
        functools.partial(_hgrn_kernel, layer=layer, chunk=chunk, t_valid=t_valid, ts=ts),
        out_shape=[
            jax.ShapeDtypeStruct((nb, s, hv), out_dtype),
            jax.ShapeDtypeStruct((nb, HG_HEADS, HG_DK, HG_DV), F32),
        ],
        grid=(nb, s // ts),
        in_specs=[
            pl.BlockSpec((None, ts, 2 * hk + 2 * hv), lambda b, i: (b, i, 0)),
            pl.BlockSpec(lb_logits.shape, lambda b, i: (0, 0)),
            pl.BlockSpec((None, HG_HEADS, HG_DK, HG_DV), lambda b, i: (b, 0, 0, 0)),
            pl.BlockSpec((1, HG_DV), lambda b, i: (0, 0)),
            pl.BlockSpec((ts, ts), lambda b, i: (0, 0)),
            pl.BlockSpec((ts, ts), lambda b, i: (0, 0)),
        ],
        out_specs=[
            pl.BlockSpec((None, ts, hv), lambda b, i: (b, i, 0)),
            pl.BlockSpec((None, HG_HEADS, HG_DK, HG_DV), lambda b, i: (b, 0, 0, 0)),
        ],
        scratch_shapes=[
            pltpu.VMEM((HG_HEADS, HG_DV, HG_DK), F32),
            pltpu.VMEM((ts, hk), F32),
            pltpu.VMEM((ts, hk), F32),
            pltpu.VMEM((ts, hk), F32),
            pltpu.VMEM((ts, hk), F32),
        ],
        compiler_params=_cparams(("arbitrary", "arbitrary")),
        name="hgrn_scan",
    )(x, lb_logits, state0, g_o.reshape(1, HG_DV), mc, ml)


def _ffn_kernel(*refs, tm, seg, stream):
    if stream:
        (x_ref, g_ref, sh_ref, sc_ref, gate_ref, wa0_ref, wan_ref, wu_ref, cw_ref, cb_ref, wd_ref,
         y_ref, tail_ref, h_sc, acc_sc, a_sc, carry_sc) = refs
    else:
        (x_ref, g_ref, sh_ref, sc_ref, gate_ref, wa0_ref, wan_ref, wu_ref, cw_ref, cb_ref, wd_ref, past_ref,
         y_ref, tail_ref, h_sc, acc_sc, a_sc) = refs
    i = pl.program_id(1)
    f = pl.program_id(2)

    @pl.when(f == 0)
    def _():
        h_sc[...] = _norm_mod(x_ref[...], g_ref[...], sh_ref[...], sc_ref[...]).astype(BF16)
        acc_sc[...] = jnp.zeros(acc_sc.shape, F32)
        a_sc[0] = _dot(h_sc[...], wa0_ref[...])

    if stream:
        @pl.when(i == 0)
        def _():
            carry_sc[f] = jnp.zeros(carry_sc.shape[1:], F32)

    def step(cur, nxt):
        h = h_sc[...]
        a = a_sc[cur]
        a_sc[nxt] = _dot(h, wan_ref[...])
        u = _dot(h, wu_ref[...])
        tf = a.shape[1]
        row = lax.broadcasted_iota(jnp.int32, (tm, tf), 0)
        a1 = pltpu.roll(a, 1, 0)
        a2 = pltpu.roll(a, 2, 0)
        if stream:
            c = carry_sc[f]
            a1 = jnp.where(row == 0, c[SUBLANES - 1:SUBLANES], a1)
            a2 = jnp.where(row == 0, c[SUBLANES - 2:SUBLANES - 1],
                           jnp.where(row == 1, c[SUBLANES - 1:SUBLANES], a2))
            carry_sc[f] = a[tm - SUBLANES:, :]
            tail_ref[...] = a[tm - SUBLANES:, :]
        else:
            t = jnp.bitwise_and(row, seg - 1)
            a1 = jnp.where(t == 0, past_ref[1], a1)
            a2 = jnp.where(t < 2, past_ref[0], a2)
            tail_ref[...] = a
        cw = cw_ref[...]
        conv = cb_ref[...] + a2 * cw[0:1] + a1 * cw[1:2] + a * cw[2:3]
        mid = (_silu(conv) * u).astype(BF16)
        acc_sc[...] += _dot(mid, wd_ref[...])

    parity = jnp.bitwise_and(f, 1)

    @pl.when(parity == 0)
    def _():
        step(0, 1)

    @pl.when(parity == 1)
    def _():
        step(1, 0)

    @pl.when(f == pl.num_programs(2) - 1)
    def _():
        y_ref[...] = x_ref[...] + gate_ref[...] * acc_sc[...]


def conv_ffn(x, g, shift, scale, gate, w_up, conv_w, conv_b, w_down, past, tm, tf):
    nb, s, d = x.shape
    dff = w_down.shape[0]
    tm = min(tm, s)
    nf = dff // tf
    stream = past is None
    in_specs = [
        pl.BlockSpec((None, tm, d), lambda b, i, f: (b, i, 0)),
        pl.BlockSpec((1, d), lambda b, i, f: (0, 0)),
        _mod_spec(shift, tm),
        _mod_spec(scale, tm),
        _mod_spec(gate, tm),
        pl.BlockSpec((d, tf), lambda b, i, f: (0, 0)),
        pl.BlockSpec((d, tf), lambda b, i, f: (0, jnp.minimum(f + 1, nf - 1))),
        pl.BlockSpec((d, tf), lambda b, i, f: (0, nf + f)),
        pl.BlockSpec((CONV_W, tf), lambda b, i, f: (0, f)),
        pl.BlockSpec((1, tf), lambda b, i, f: (0, f)),
        pl.BlockSpec((tf, d), lambda b, i, f: (f, 0)),
    ]
    args = [x, g.reshape(1, d), shift, scale, gate, w_up, w_up, w_up, conv_w, conv_b.reshape(1, dff), w_down]
    scratch = [pltpu.VMEM((tm, d), BF16), pltpu.VMEM((tm, d), F32), pltpu.VMEM((2, tm, tf), F32)]
    if stream:
        tail_shape = jax.ShapeDtypeStruct((nb, s // tm, SUBLANES, dff), F32)
        tail_spec = pl.BlockSpec((None, None, SUBLANES, tf), lambda b, i, f: (b, i, 0, f))
        scratch.append(pltpu.VMEM((nf, SUBLANES, tf), F32))
    else:
        in_specs.append(pl.BlockSpec((CONV_W - 1, tm, tf), lambda b, i, f: (0, i, f)))
        args.append(past)
        tail_shape = jax.ShapeDtypeStruct((nb, s, dff), F32)
        tail_spec = pl.BlockSpec((None, tm, tf), lambda b, i, f: (b, i, f))
    return pl.pallas_call(
        functools.partial(_ffn_kernel, tm=tm, seg=T_PAD, stream=stream),
        out_shape=[jax.ShapeDtypeStruct((nb, s, d), F32), tail_shape],
        grid=(nb, s // tm, nf),
        in_specs=in_specs,
        out_specs=[pl.BlockSpec((None, tm, d), lambda b, i, f: (b, i, 0)), tail_spec],
        scratch_shapes=scratch,
        compiler_params=_cparams(("arbitrary", "arbitrary", "arbitrary")),
        name="conv_ffn",
    )(*args)


def _absorb_kernel(q_ref, wk_ref, gk_ref, qa_ref, qp_ref):
    gk = gk_ref[...]
    for hd in range(MLA_HEADS):
        c0 = 2 * LANES * hd
        qn = (q_ref[:, c0:c0 + LANES] * gk[:, :LANES]).astype(BF16)
        qa_ref[:, c0:c0 + 2 * LANES] = _dot_nt(qn, wk_ref[:, LANES * hd:LANES * (hd + 1)]).astype(qa_ref.dtype)
        qp_ref[:, LANES * hd:LANES * (hd + 1)] = (q_ref[:, c0 + LANES:c0 + 2 * LANES] * gk[:, LANES:]).astype(qp_ref.dtype)


def mla_absorb(q, wk, gk):
    m = q.shape[0]
    return pl.pallas_call(
        _absorb_kernel,
        out_shape=[jax.ShapeDtypeStruct((m, MLA_HEADS * KV_LORA), BF16),
                   jax.ShapeDtypeStruct((m, MLA_HEADS * LANES), BF16)],
        name="mla_absorb",
    )(q, wk, gk)


def _vup_kernel(o_ref, wv_ref, out_ref):
    for hd in range(MLA_HEADS):
        lat = o_ref[:, KV_LORA * hd:KV_LORA * (hd + 1)].astype(BF16)
        out_ref[:, V_HEAD * hd:V_HEAD * (hd + 1)] = _dot(lat, wv_ref[:, V_HEAD * hd:V_HEAD * (hd + 1)])


def mla_vup(o_lat, wv):
    m = o_lat.shape[0]
    return pl.pallas_call(
        _vup_kernel,
        out_shape=jax.ShapeDtypeStruct((m, MLA_HEADS * V_HEAD), F32),
        name="mla_vup",
    )(o_lat, wv)


def _mla_dec_kernel(pt_ref, new_ref, *refs, n_slots):
    page_refs = refs[:n_slots]
    qa_ref, qp_ref, wkt_ref, o_ref, m_sc, l_sc, acc_sc = refs[n_slots:]
    s = pl.program_id(1)
    n_rows = T_PAD * MLA_HEADS

    def scores(tile):
        cb = tile[:KV_LORA].astype(BF16)
        kpt = tile[KV_LORA:]
        kt = _dot(wkt_ref[...], cb)
        ss_pe = jnp.sum(kpt * kpt, axis=0, keepdims=True)
        ss = jnp.concatenate(
            [jnp.sum(kt[QK_NOPE * hd:QK_NOPE * (hd + 1)] ** 2, axis=0, keepdims=True) for hd in range(MLA_HEADS)],
            axis=0)
        r = lax.rsqrt((ss + ss_pe) * (1.0 / QK_HEAD) + EPS)
        sc = _dot(qa_ref[...], cb) + _dot(qp_ref[...], kpt.astype(BF16))
        return sc * jnp.concatenate([r] * T_PAD, axis=0), cb

    def update(scs, cbs):
        m_prev = m_sc[...]
        m_new = m_prev
        for sc in scs:
            m_new = jnp.maximum(m_new, jnp.max(sc, axis=-1, keepdims=True))
        alpha = jnp.exp(m_prev - m_new)
        l_new = alpha * l_sc[...]
        acc = alpha * acc_sc[...]
        for sc, cb in zip(scs, cbs):
            pr = jnp.exp(sc - m_new)
            l_new = l_new + jnp.sum(pr, axis=-1, keepdims=True)
            acc = acc + _dot_nt(pr.astype(BF16), cb)
        l_sc[...] = l_new
        acc_sc[...] = acc
        m_sc[...] = m_new

    @pl.when(s == 0)
    def _():
        m_sc[...] = jnp.full((n_rows, 1), NEG_BIG, F32)
        l_sc[...] = jnp.zeros((n_rows, 1), F32)
        acc_sc[...] = jnp.zeros((n_rows, KV_LORA), F32)
        sc, cb = scores(new_ref[...])
        tok = lax.shift_right_logical(lax.broadcasted_iota(jnp.int32, sc.shape, 0), HEAD_SHIFT)
        key = lax.broadcasted_iota(jnp.int32, sc.shape, 1)
        update([jnp.where(key <= tok, sc, NEG_BIG)], [cb])

    @pl.when(s > 0)
    def _():
        pairs = [scores(jnp.concatenate([page_refs[a][...], page_refs[a + 1][...]], axis=1))
                 for a in range(0, n_slots, 2)]
        update([p[0] for p in pairs], [p[1] for p in pairs])

    @pl.when(s == pl.num_programs(1) - 1)
    def _():
        o_ref[...] = acc_sc[...] / l_sc[...]


def mla_decode(page_table, new_page, cache_t, layer, qa, qp, wk_t, n_slots):
    db, n_pages = page_table.shape
    n_rows = T_PAD * MLA_HEADS
    row_w = cache_t.shape[2]
    assert n_slots % 2 == 0 and n_pages % n_slots == 0
    steps = n_pages // n_slots

    def page_spec(slot):
        def imap(b, s, pt):
            idx = jnp.maximum(s - 1, 0) * n_slots + slot
            return (layer, pt[b * n_pages + idx], 0, 0)
        return pl.BlockSpec((None, None, row_w, PAGE_SIZE), imap)

    grid_spec = pltpu.PrefetchScalarGridSpec(
        num_scalar_prefetch=1,
        grid=(db, steps + 1),
        in_specs=[pl.BlockSpec((None, row_w, PAGE_SIZE), lambda b, s, pt: (b, 0, 0))]
        + [page_spec(i) for i in range(n_slots)]
        + [
            pl.BlockSpec((None, n_rows, KV_LORA), lambda b, s, pt: (b, 0, 0)),
            pl.BlockSpec((None, n_rows, QK_ROPE), lambda b, s, pt: (b, 0, 0)),
            pl.BlockSpec(wk_t.shape, lambda b, s, pt: (0, 0)),
        ],
        out_specs=pl.BlockSpec((None, n_rows, KV_LORA), lambda b, s, pt: (b, 0, 0)),
        scratch_shapes=[pltpu.VMEM((n_rows, 1), F32), pltpu.VMEM((n_rows, 1), F32),
                        pltpu.VMEM((n_rows, KV_LORA), F32)],
    )
    return pl.pallas_call(
        functools.partial(_mla_dec_kernel, n_slots=n_slots),
        out_shape=jax.ShapeDtypeStruct((db, n_rows, KV_LORA), F32),
        grid_spec=grid_spec,
        compiler_params=_cparams(("arbitrary", "arbitrary")),
        name="mla_decode",
    )(page_table.reshape(-1), new_page, *([cache_t] * n_slots), qa, qp, wk_t)


def _sb_dec_kernel(pt_ref, new_ref, *refs, n_slots):
    page_refs = refs[:n_slots]
    q_ref, u_ref, un_ref, shn_ref, o_ref, keep_sc, acc_sc = refs[n_slots:]
    s = pl.program_id(1)
    n_rows = T_PAD * SB_HEADS

    def process(pages, n_keys, causal, cdt, keep, acc, later_of, shift_of):
        w = 2 * n_keys
        qs = [(q_ref[:, SB_HEAD * hd:SB_HEAD * (hd + 1)] * SB_SCALE).astype(cdt) for hd in range(SB_HEADS)]
        kvs = [[ref[pl.ds(hd, w, stride=SB_HEADS), :].astype(cdt) for hd in range(SB_HEADS)] for ref in pages]
        z = jnp.concatenate(
            [jnp.concatenate([_dot_nt(qs[hd], kv[hd]) for hd in range(SB_HEADS)], axis=0) for kv in kvs],
            axis=1)
        log_keep, log_break = _sb_terms(z)
        lane = lax.broadcasted_iota(jnp.int32, z.shape, 1)
        mask = jnp.bitwise_and(lane, 1) == 0
        if causal:
            tok = jnp.bitwise_and(lax.broadcasted_iota(jnp.int32, z.shape, 0), T_PAD - 1)
            mask = mask & (lax.shift_right_logical(lane, 1) < tok)
        log_keep = jnp.where(mask, log_keep, 0.0)
        later = []
        for p in range(len(pages)):
            lk = log_keep[:, w * p:w * (p + 1)]
            later.append(later_of(lk) + keep)
            keep = keep + jnp.sum(lk, axis=-1, keepdims=True)
        a = jnp.where(mask, jnp.exp(log_break + jnp.concatenate(later, axis=1)), 0.0)
        a = shift_of(a)
        outs = []
        for hd in range(SB_HEADS):
            rows = slice(T_PAD * hd, T_PAD * (hd + 1))
            o = None
            for p, kv in enumerate(kvs):
                d = _dot(a[rows, w * p:w * (p + 1)].astype(cdt), kv[hd])
                o = d if o is None else o + d
            outs.append(o)
        return keep, acc + jnp.concatenate(outs, axis=0)

    @pl.when(s == 0)
    def _():
        exact = functools.partial(jnp.dot, precision=lax.Precision.HIGHEST, preferred_element_type=F32)
        keep, acc = process([new_ref], T_PAD, True, F32, jnp.zeros((n_rows, 1), F32),
                            jnp.zeros((n_rows, SB_HEAD), F32),
                            lambda lk: exact(lk, un_ref[...]), lambda a: exact(a, shn_ref[...]))
        keep_sc[...] = keep
        acc_sc[...] = acc

    @pl.when(s > 0)
    def _():
        keep, acc = process(page_refs, PAGE_SIZE, False, BF16, keep_sc[...], acc_sc[...],
                            lambda lk: _dot01_right(lk, u_ref[...]),
                            lambda a: pltpu.roll(a, 1, 1))
        keep_sc[...] = keep
        acc_sc[...] = acc

    @pl.when(s == pl.num_programs(1) - 1)
    def _():
        for hd in range(SB_HEADS):
            o_ref[:, SB_HEAD * hd:SB_HEAD * (hd + 1)] = acc_sc[T_PAD * hd:T_PAD * (hd + 1), :]


def sb_decode(page_table, new_page, cache_rows, layer, q, n_slots):
    db, n_pages = page_table.shape
    n_rows = T_PAD * SB_HEADS
    hw = SB_HEADS * SB_HEAD
    page_rows = cache_rows.shape[2]
    assert n_pages % n_slots == 0
    steps = n_pages // n_slots

    def page_spec(slot):
        def imap(b, s, pt):
            idx = n_pages - 1 - (jnp.maximum(s - 1, 0) * n_slots + slot)
            return (layer, pt[b * n_pages + idx], 0, 0)
        return pl.BlockSpec((None, None, page_rows, SB_HEAD), imap)

    grid_spec = pltpu.PrefetchScalarGridSpec(
        num_scalar_prefetch=1,
        grid=(db, steps + 1),
        in_specs=[pl.BlockSpec((None,) + new_page.shape[1:], lambda b, s, pt: (b, 0, 0))]
        + [page_spec(i) for i in range(n_slots)]
        + [pl.BlockSpec((None, T_PAD, hw), lambda b, s, pt: (b, 0, 0))]
        + [pl.BlockSpec((2 * n, 2 * n), lambda b, s, pt: (0, 0)) for n in (PAGE_SIZE, T_PAD, T_PAD)],
        out_specs=pl.BlockSpec((None, T_PAD, hw), lambda b, s, pt: (b, 0, 0)),
        scratch_shapes=[pltpu.VMEM((n_rows, 1), F32), pltpu.VMEM((n_rows, SB_HEAD), F32)],
    )
    return pl.pallas_call(
        functools.partial(_sb_dec_kernel, n_slots=n_slots),
        out_shape=jax.ShapeDtypeStruct((db, T_PAD, hw), F32),
        grid_spec=grid_spec,
        compiler_params=_cparams(("arbitrary", "arbitrary")),
        name="sb_decode",
    )(page_table.reshape(-1), new_page, *([cache_rows] * n_slots), q,
      _interleaved_matrices(PAGE_SIZE, BF16)[0], *_interleaved_matrices(T_PAD, F32))


def _interleaved_matrices(n_keys, dtype):
    r = lax.broadcasted_iota(jnp.int32, (2 * n_keys, 2 * n_keys), 0)
    c = lax.broadcasted_iota(jnp.int32, (2 * n_keys, 2 * n_keys), 1)
    return ((r // 2) > (c // 2)).astype(dtype), (c == r + 1).astype(dtype)


TM_LINEAR = 512
TM_FFN = 1024
TF_FFN = 256
TQ_MLA, TK_MLA = 512, 512
TQ_SB, TK_SB = 512, 256
ATTN_HEADS_PER_STEP = 2
ATTN_ROW_BLOCK = 256
TS_HGRN = 256
MLA_PAGES_PER_STEP = 8
SB_PAGES_PER_STEP = 8


def _pad_tokens(x, t_pad):
    return jnp.pad(x, ((0, 0), (0, t_pad - x.shape[1])) + ((0, 0),) * (x.ndim - 2))


def kernel(x_prompt, x_sample, c_prompt, c_sample, page_table, cache_mla, cache_sb_kv, state_hgrn,
           state_ffn_conv, w_mod, b_mod, norm_g, w_mla_a, g_mla_qa, g_mla_kva, w_mla_uq, w_mla_ukv,
           g_mla_q, g_mla_k, w_mla_o, w_sb_qkv, w_sb_o, w_hg_in, hg_lb_logits, g_hg_o, w_hg_o,
           w_ffn_up, ffn_conv_w, ffn_conv_b, w_ffn_down):
    b, s_len, d = x_prompt.shape
    db, t, _ = x_sample.shape
    depth = w_mod.shape[0]
    n_pages = page_table.shape[1]
    past = n_pages * PAGE_SIZE
    dff = w_ffn_down.shape[1]
    assert t <= T_PAD and t >= CONV_W - 1

    mod = ada_mod(jnp.concatenate([c_prompt, c_sample], axis=0), w_mod, b_mod)
    mod = mod.reshape(depth, b + db, N_MOD, d)
    cos_p, sin_p = _rope_tables(jnp.arange(s_len))
    cos_s, sin_s = _rope_tables(jnp.tile(past + jnp.arange(T_PAD), db))
    sb_cache = cache_sb_kv.reshape(cache_sb_kv.shape[:2] + (PAGE_SIZE * 2 * SB_HEADS, SB_HEAD))
    mla_cache_t = jnp.swapaxes(cache_mla, 2, 3)

    xp = x_prompt
    xs = _pad_tokens(x_sample, T_PAD).reshape(1, db * T_PAD, d)
    mla_p, mla_s, sb_p, sb_s, hg_p, hg_s, ffn_p, ffn_s = [], [], [], [], [], [], [], []
    for i in range(depth):
        mp = [mod[i, :b, j].reshape(b, 1, d) for j in range(N_MOD)]
        ms = [jnp.repeat(mod[i, b:, j], T_PAD, axis=0).reshape(1, db * T_PAD, d) for j in range(N_MOD)]
        kind, j = i % N_MIXERS, i // N_MIXERS
        if kind == 0:
            wp = _mla_weights(w_mla_a[j], g_mla_qa[j], g_mla_kva[j], w_mla_uq[j], w_mla_ukv[j],
                              g_mla_q[j], g_mla_k[j])
            rows_p, qp, kp, vp = mla_proj(xp, norm_g[i, 0], mp[0], mp[1], cos_p, sin_p, wp, BF16, TM_LINEAR)
            op = flash_mla(qp, kp, vp, TQ_MLA, TK_MLA, ATTN_HEADS_PER_STEP, ATTN_ROW_BLOCK)
            rows_s, qs, _, _ = mla_proj(xs, norm_g[i, 0], ms[0], ms[1], cos_s, sin_s, wp, F32, TM_LINEAR)
            qa, qpe = mla_absorb(qs[0], wp["wk"], wp["gk"])
            qa = qa.reshape(db, T_PAD * MLA_HEADS, KV_LORA)
            qpe = qpe.reshape(db, T_PAD * MLA_HEADS, LANES)[:, :, :QK_ROPE]
            rows_s = rows_s.reshape(db, T_PAD, -1)
            new_page = jnp.swapaxes(_pad_tokens(rows_s, PAGE_SIZE), 1, 2)
            o_lat = mla_decode(page_table, new_page, mla_cache_t, j, qa, qpe, wp["wk_t"], MLA_PAGES_PER_STEP)
            os_ = mla_vup(o_lat.reshape(db * T_PAD, MLA_HEADS * KV_LORA), wp["wv"])
            os_ = os_.reshape(1, db * T_PAD, -1)
            w_o = w_mla_o[j]
            mla_p.append(rows_p)
            mla_s.append(rows_s[:, :t])
        elif kind == 1:
            wq = w_sb_qkv[j].astype(BF16)
            hw = SB_HEADS * SB_HEAD
            q_p, kv_p = norm_mod_linear(xp, norm_g[i, 0], mp[0], mp[1], [wq[:, :hw], wq[:, hw:]], TM_LINEAR)
            op = sb_attention(q_p, kv_p, TQ_SB, TK_SB, ATTN_HEADS_PER_STEP, ATTN_ROW_BLOCK)
            q_s, kv_s = norm_mod_linear(xs, norm_g[i, 0], ms[0], ms[1], [wq[:, :hw], wq[:, hw:]], TM_LINEAR)
            kv_s = kv_s.reshape(db, T_PAD, 2 * hw)
            new_page = kv_s.reshape(db, T_PAD * 2 * SB_HEADS, SB_HEAD)
            os_ = sb_decode(page_table, new_page, sb_cache, j, q_s.reshape(db, T_PAD, hw), SB_PAGES_PER_STEP)
            os_ = os_.reshape(1, db * T_PAD, hw)
            w_o = w_sb_o[j]
            sb_p.append(kv_p.reshape(b, s_len, 2, SB_HEADS, SB_HEAD))
            sb_s.append(kv_s[:, :t].reshape(db, t, 2, SB_HEADS, SB_HEAD))
        else:
            w_in = w_hg_in[j].astype(BF16)
            (zp,) = norm_mod_linear(xp, norm_g[i, 0], mp[0], mp[1], [w_in], TM_LINEAR)
            zero_state = jnp.zeros((b, HG_HEADS, HG_DK, HG_DV), F32)
            op, st_p = hgrn_scan(zp, hg_lb_logits, zero_state, g_hg_o[j], i, HG_CHUNK, s_len, TS_HGRN, BF16)
            (zs,) = norm_mod_linear(xs, norm_g[i, 0], ms[0], ms[1], [w_in], TM_LINEAR)
            os_, st_s = hgrn_scan(zs.reshape(db, T_PAD, -1), hg_lb_logits, state_hgrn[j], g_hg_o[j], i,
                                  T_PAD, t, T_PAD, F32)
            os_ = os_.reshape(1, db * T_PAD, -1)
            w_o = w_hg_o[j]
            hg_p.append(st_p)
            hg_s.append(st_s)
        w_o = w_o.astype(BF16)
        xp = proj_residual(op, w_o, xp, mp[2], TM_LINEAR)
        xs = proj_residual(os_, w_o, xs, ms[2], TM_LINEAR)

        w_up = w_ffn_up[i].astype(BF16)
        w_down = w_ffn_down[i].astype(BF16)
        xp, tail_p = conv_ffn(xp, norm_g[i, 1], mp[3], mp[4], mp[5], w_up, ffn_conv_w[i], ffn_conv_b[i],
                              w_down, None, TM_FFN, TF_FFN)
        st = state_ffn_conv[i]
        z = jnp.zeros((db, T_PAD - 2, dff), F32)
        past2 = jnp.concatenate([st, z], axis=1)
        past1 = jnp.concatenate([st[:, 1:], z, z[:, :1]], axis=1)
        past_rows = jnp.stack([past2, past1]).reshape(CONV_W - 1, db * T_PAD, dff)
        xs, a_s = conv_ffn(xs, norm_g[i, 1], ms[3], ms[4], ms[5], w_up, ffn_conv_w[i], ffn_conv_b[i],
                           w_down, past_rows, TM_FFN, TF_FFN)
        ffn_p.append(tail_p[:, -1, SUBLANES - (CONV_W - 1):])
        ffn_s.append(a_s.reshape(db, T_PAD, dff)[:, t - (CONV_W - 1):t])
    ys = xs.reshape(db, T_PAD, d)[:, :t]
    return (xp, ys, jnp.stack(mla_p), jnp.stack(mla_s), jnp.stack(sb_p), jnp.stack(sb_s),
            jnp.stack(hg_p), jnp.stack(hg_s), jnp.stack(ffn_p), jnp.stack(ffn_s))
```

```python
import functools

import jax
import jax.numpy as jnp
import numpy as np
from jax import lax
from jax.experimental import pallas as pl
from jax.experimental.pallas import tpu as pltpu

F32 = jnp.float32
BF16 = jnp.bfloat16

N_MIXERS = 3
MLA_HEADS = 8
QK_NOPE = 128
QK_ROPE = 64
V_HEAD = 128
Q_LORA = 256
KV_LORA = 256
QK_HEAD = QK_NOPE + QK_ROPE
MLA_SCALE = QK_HEAD ** -0.5
ROPE_BASE = 10000.0
SB_HEADS = 8
SB_HEAD = 128
SB_SCALE = SB_HEAD ** -0.5
HG_HEADS = 8
HG_DK = 128
HG_DV = 128
HG_CHUNK = 32
CONV_W = 3
PAGE_SIZE = 128
EPS = 1e-6
N_MOD = 6

LANES = 128
SUBLANES = 8
T_PAD = SUBLANES
NEG_BIG = -1e30
HEAD_SHIFT = 3
LANE_SHIFT = 7
VMEM_LIMIT = 48 * 1024 * 1024


def _cparams(sem):
    return pltpu.CompilerParams(dimension_semantics=sem, vmem_limit_bytes=VMEM_LIMIT)


def _dot(a, b):
    return jnp.dot(a, b, preferred_element_type=F32)


def _dot_nt(a, b):
    return lax.dot_general(a, b, (((1,), (1,)), ((), ())), preferred_element_type=F32)


def _dot_tn(a, b):
    return lax.dot_general(a, b, (((0,), (0,)), ((), ())), preferred_element_type=F32)


def _split_bf16(x, parts):
    out = []
    r = x
    for _ in range(parts):
        p = r.astype(BF16)
        out.append(p)
        r = r - p.astype(F32)
    return out


def _dot01(m01, x, parts=3):
    acc = None
    for p in _split_bf16(x, parts):
        d = _dot(m01, p)
        acc = d if acc is None else acc + d
    return acc


def _dot01_right(x, m01, parts=2):
    acc = None
    for p in _split_bf16(x, parts):
        d = _dot(p, m01)
        acc = d if acc is None else acc + d
    return acc


def _rms(x, g):
    ms = jnp.mean(x * x, axis=-1, keepdims=True)
    return x * lax.rsqrt(ms + EPS) * g


def _norm_mod(x, g, shift, scale):
    return _rms(x, g) * (1.0 + scale) + shift


def _silu(x):
    return x * jax.nn.sigmoid(x)


def _mod_kernel(c_ref, w_ref, b_ref, o_ref):
    s = _silu(c_ref[...])
    o_ref[...] = _dot(s.astype(BF16), w_ref[...].astype(BF16)) + b_ref[...]


def ada_mod(c_all, w_mod, b_mod):
    depth, d, n = w_mod.shape
    nc = c_all.shape[0]
    tn = n // 4
    return pl.pallas_call(
        _mod_kernel,
        out_shape=jax.ShapeDtypeStruct((depth, nc, n), F32),
        grid=(depth, n // tn),
        in_specs=[
            pl.BlockSpec((nc, d), lambda l, j: (0, 0)),
            pl.BlockSpec((None, d, tn), lambda l, j: (l, 0, j)),
            pl.BlockSpec((None, 1, tn), lambda l, j: (l, 0, j)),
        ],
        out_specs=pl.BlockSpec((None, nc, tn), lambda l, j: (l, 0, j)),
        compiler_params=_cparams(("arbitrary", "arbitrary")),
        name="ada_mod",
    )(c_all, w_mod, b_mod.reshape(depth, 1, n))


def _mod_spec(mod, tm):
    r = mod.shape[1]
    if r == 1:
        return pl.BlockSpec((None, 1, mod.shape[2]), lambda b, i, *_: (b, 0, 0))
    return pl.BlockSpec((None, tm, mod.shape[2]), lambda b, i, *_: (b, i, 0))


def _nml_kernel(x_ref, g_ref, sh_ref, sc_ref, *rest, n_w):
    h = _norm_mod(x_ref[...], g_ref[...], sh_ref[...], sc_ref[...]).astype(BF16)
    for w_ref, o_ref in zip(rest[:n_w], rest[n_w:]):
        o_ref[...] = _dot(h, w_ref[...]).astype(o_ref.dtype)


def norm_mod_linear(x, g, shift, scale, ws, tm):
    nb, s, d = x.shape
    tm = min(tm, s)
    in_specs = [
        pl.BlockSpec((None, tm, d), lambda b, i: (b, i, 0)),
        pl.BlockSpec((1, d), lambda b, i: (0, 0)),
        _mod_spec(shift, tm),
        _mod_spec(scale, tm),
    ]
    out_shape, out_specs = [], []
    for w in ws:
        in_specs.append(pl.BlockSpec(w.shape, lambda b, i: (0, 0)))
        out_shape.append(jax.ShapeDtypeStruct((nb, s, w.shape[1]), F32))
        out_specs.append(pl.BlockSpec((None, tm, w.shape[1]), lambda b, i: (b, i, 0)))
    return pl.pallas_call(
        functools.partial(_nml_kernel, n_w=len(ws)),
        out_shape=out_shape,
        grid=(nb, s // tm),
        in_specs=in_specs,
        out_specs=out_specs,
        compiler_params=_cparams(("arbitrary", "arbitrary")),
        name="norm_mod_linear",
    )(x, g.reshape(1, d), shift, scale, *ws)


def _proj_res_kernel(o_ref, w_ref, x_ref, gate_ref, out_ref):
    y = _dot(o_ref[...].astype(BF16), w_ref[...])
    out_ref[...] = x_ref[...] + gate_ref[...] * y


def proj_residual(o, w, x, gate, tm):
    nb, s, d = x.shape
    din = o.shape[2]
    tm = min(tm, s)
    return pl.pallas_call(
        _proj_res_kernel,
        out_shape=jax.ShapeDtypeStruct((nb, s, d), F32),
        grid=(nb, s // tm),
        in_specs=[
            pl.BlockSpec((None, tm, din), lambda b, i: (b, i, 0)),
            pl.BlockSpec((din, d), lambda b, i: (0, 0)),
            pl.BlockSpec((None, tm, d), lambda b, i: (b, i, 0)),
            _mod_spec(gate, tm),
        ],
        out_specs=pl.BlockSpec((None, tm, d), lambda b, i: (b, i, 0)),
        compiler_params=_cparams(("arbitrary", "arbitrary")),
        name="proj_residual",
    )(o, w, x, gate)


def _mla_proj_kernel(x_ref, g_ref, sh_ref, sc_ref, cos_ref, sin_ref, wa_ref, gqa_ref, gkva_ref,
                     wqm_ref, wqs_ref, wkv_ref, gq_ref, gk_ref,
                     rows_ref, q_ref, k_ref, v_ref):
    h = _norm_mod(x_ref[...], g_ref[...], sh_ref[...], sc_ref[...]).astype(BF16)
    a = _dot(h, wa_ref[...])
    cos = cos_ref[...]
    sin = sin_ref[...]
    qn = _rms(a[:, :Q_LORA], gqa_ref[...]).astype(BF16)
    cn = _rms(a[:, Q_LORA:Q_LORA + KV_LORA], gkva_ref[...])
    o = Q_LORA + KV_LORA
    kpe = a[:, o:o + LANES] * cos + a[:, o + LANES:o + 2 * LANES] * sin
    rows_ref[...] = jnp.concatenate([cn, kpe[:, :QK_ROPE]], axis=-1)
    qm = _dot(qn, wqm_ref[...])
    qs = _dot(qn, wqs_ref[...])
    kv = _dot(cn.astype(BF16), wkv_ref[...])
    gq = gq_ref[...]
    gk = gk_ref[...]
    kpe_ss = jnp.sum(kpe * kpe, axis=-1, keepdims=True)
    inv = 1.0 / QK_HEAD
    for hd in range(MLA_HEADS):
        c0 = 2 * LANES * hd
        nope = qm[:, c0:c0 + LANES]
        pe = qm[:, c0 + LANES:c0 + 2 * LANES] * cos + qs[:, LANES * hd:LANES * (hd + 1)] * sin
        ss = jnp.sum(nope * nope, axis=-1, keepdims=True) + jnp.sum(pe * pe, axis=-1, keepdims=True)
        r = lax.rsqrt(ss * inv + EPS) * MLA_SCALE
        q_ref[:, c0:c0 + LANES] = (nope * r * gq[:, :LANES]).astype(q_ref.dtype)
        q_ref[:, c0 + LANES:c0 + 2 * LANES] = (pe * r * gq[:, LANES:]).astype(q_ref.dtype)
        kn = kv[:, c0:c0 + LANES]
        rk = lax.rsqrt((jnp.sum(kn * kn, axis=-1, keepdims=True) + kpe_ss) * inv + EPS)
        k_ref[:, c0:c0 + LANES] = (kn * rk * gk[:, :LANES]).astype(k_ref.dtype)
        k_ref[:, c0 + LANES:c0 + 2 * LANES] = (kpe * rk * gk[:, LANES:]).astype(k_ref.dtype)
        v_ref[:, LANES * hd:LANES * (hd + 1)] = kv[:, c0 + LANES:c0 + 2 * LANES].astype(v_ref.dtype)


def mla_proj(x, g, shift, scale, cos, sin, wp, q_dtype, tm):
    nb, s, d = x.shape
    tm = min(tm, s)
    hw = 2 * LANES * MLA_HEADS
    full = lambda arr: pl.BlockSpec(arr.shape, lambda b, i: (0, 0))
    return pl.pallas_call(
        _mla_proj_kernel,
        out_shape=[
            jax.ShapeDtypeStruct((nb, s, KV_LORA + QK_ROPE), F32),
            jax.ShapeDtypeStruct((nb, s, hw), q_dtype),
            jax.ShapeDtypeStruct((nb, s, hw), BF16),
            jax.ShapeDtypeStruct((nb, s, LANES * MLA_HEADS), BF16),
        ],
        grid=(nb, s // tm),
        in_specs=[
            pl.BlockSpec((None, tm, d), lambda b, i: (b, i, 0)),
            pl.BlockSpec((1, d), lambda b, i: (0, 0)),
            _mod_spec(shift, tm),
            _mod_spec(scale, tm),
            pl.BlockSpec((tm, LANES), lambda b, i: (i, 0)),
            pl.BlockSpec((tm, LANES), lambda b, i: (i, 0)),
            full(wp["wa"]), full(wp["gqa"]), full(wp["gkva"]), full(wp["wqm"]), full(wp["wqs"]),
            full(wp["wkv"]), full(wp["gq"]), full(wp["gk"]),
        ],
        out_specs=[
            pl.BlockSpec((None, tm, KV_LORA + QK_ROPE), lambda b, i: (b, i, 0)),
            pl.BlockSpec((None, tm, hw), lambda b, i: (b, i, 0)),
            pl.BlockSpec((None, tm, hw), lambda b, i: (b, i, 0)),
            pl.BlockSpec((None, tm, LANES * MLA_HEADS), lambda b, i: (b, i, 0)),
        ],
        compiler_params=_cparams(("arbitrary", "arbitrary")),
        name="mla_proj",
    )(x, g.reshape(1, d), shift, scale, cos, sin, wp["wa"], wp["gqa"], wp["gkva"], wp["wqm"], wp["wqs"],
      wp["wkv"], wp["gq"], wp["gk"])


def _mla_weights(w_a, g_qa, g_kva, w_uq, w_ukv, g_q, g_k):
    d = w_a.shape[0]
    half = QK_ROPE // 2
    zpad = jnp.zeros((d, LANES - QK_ROPE), F32)
    w_pe = w_a[:, Q_LORA + KV_LORA:]
    w_pe_sw = jnp.concatenate([w_pe[:, half:], w_pe[:, :half]], axis=1)
    wa = jnp.concatenate([w_a[:, :Q_LORA + KV_LORA], w_pe, zpad, w_pe_sw, zpad], axis=1).astype(BF16)
    wq = w_uq.reshape(Q_LORA, MLA_HEADS, QK_HEAD)
    wq_pe = wq[:, :, QK_NOPE:]
    z = jnp.zeros((Q_LORA, MLA_HEADS, LANES - QK_ROPE), F32)
    wqm = jnp.concatenate([wq, z], axis=2).reshape(Q_LORA, MLA_HEADS * 2 * LANES).astype(BF16)
    wq_sw = jnp.concatenate([wq_pe[:, :, half:], wq_pe[:, :, :half], z], axis=2)
    wqs = wq_sw.reshape(Q_LORA, MLA_HEADS * LANES).astype(BF16)
    gpad = jnp.zeros((2 * LANES - QK_HEAD,), F32)
    wkv3 = w_ukv.reshape(KV_LORA, MLA_HEADS, QK_NOPE + V_HEAD)
    return {
        "wa": wa,
        "gqa": g_qa.reshape(1, Q_LORA),
        "gkva": g_kva.reshape(1, KV_LORA),
        "wqm": wqm,
        "wqs": wqs,
        "wkv": w_ukv.astype(BF16),
        "gq": jnp.concatenate([g_q, gpad]).reshape(1, 2 * LANES),
        "gk": jnp.concatenate([g_k, gpad]).reshape(1, 2 * LANES),
        "wk_t": wkv3[:, :, :QK_NOPE].reshape(KV_LORA, MLA_HEADS * QK_NOPE).T.astype(BF16),
        "wk": wkv3[:, :, :QK_NOPE].reshape(KV_LORA, MLA_HEADS * QK_NOPE).astype(BF16),
        "wv": wkv3[:, :, QK_NOPE:].reshape(KV_LORA, MLA_HEADS * V_HEAD).astype(BF16),
    }


def _rope_tables(pos):
    half = QK_ROPE // 2
    inv = ROPE_BASE ** (-jnp.arange(half, dtype=F32) / half)
    ang = pos.astype(F32)[:, None] * inv[None, :]
    cos, sin = jnp.cos(ang), jnp.sin(ang)
    z = jnp.zeros((pos.shape[0], LANES - QK_ROPE), F32)
    return jnp.concatenate([cos, cos, z], axis=1), jnp.concatenate([-sin, sin, z], axis=1)


def _pairs(n_q, tq, tk, newest_first):
    qi, ki = [], []
    for a in range(n_q):
        last = ((a + 1) * tq - 1) // tk
        ks = range(last, -1, -1) if newest_first else range(last + 1)
        for b in ks:
            qi.append(a)
            ki.append(b)
    return jnp.asarray(qi, jnp.int32), jnp.asarray(ki, jnp.int32)


def _flash_kernel(qi_ref, ki_ref, q_ref, k_ref, v_ref, o_ref, m_sc, l_sc, acc_sc, *, tq, tk, hps, tr):
    p = pl.program_id(2)
    qi = qi_ref[p]
    ki = ki_ref[p]

    @pl.when(ki == 0)
    def _():
        m_sc[...] = jnp.full(m_sc.shape, NEG_BIG, F32)
        l_sc[...] = jnp.zeros(l_sc.shape, F32)
        acc_sc[...] = jnp.zeros(acc_sc.shape, F32)

    def update(masked):
        for hh in range(hps):
            qk = slice(2 * LANES * hh, 2 * LANES * (hh + 1))
            vc = slice(LANES * hh, LANES * (hh + 1))
            for r0 in range(0, tq, tr):
                rows = slice(r0, r0 + tr)
                s = _dot_nt(q_ref[rows, qk], k_ref[:, qk])
                if masked:
                    row = qi * tq + r0 + lax.broadcasted_iota(jnp.int32, (tr, tk), 0)
                    col = ki * tk + lax.broadcasted_iota(jnp.int32, (tr, tk), 1)
                    s = jnp.where(row >= col, s, NEG_BIG)
                m_prev = m_sc[hh, rows]
                m_new = jnp.maximum(m_prev, jnp.max(s, axis=-1, keepdims=True))
                alpha = jnp.exp(m_prev - m_new)
                pr = jnp.exp(s - m_new)
                l_sc[hh, rows] = alpha * l_sc[hh, rows] + jnp.sum(pr, axis=-1, keepdims=True)
                acc_sc[rows, vc] = alpha * acc_sc[rows, vc] + _dot(pr.astype(BF16), v_ref[:, vc])
                m_sc[hh, rows] = m_new

    on_diagonal = (ki + 1) * tk > qi * tq

    @pl.when(on_diagonal)
    def _():
        update(True)

    @pl.when(jnp.logical_not(on_diagonal))
    def _():
        update(False)

    @pl.when(ki == ((qi + 1) * tq - 1) // tk)
    def _():
        for hh in range(hps):
            vc = slice(LANES * hh, LANES * (hh + 1))
            o_ref[:, vc] = (acc_sc[:, vc] / l_sc[hh]).astype(o_ref.dtype)


def flash_mla(q, k, v, tq, tk, hps, tr):
    nb, s, _ = q.shape
    tq, tk = min(tq, s), min(tk, s)
    tr = min(tr, tq)
    assert MLA_HEADS % hps == 0 and tq % tr == 0
    qi, ki = _pairs(s // tq, tq, tk, newest_first=False)
    grid_spec = pltpu.PrefetchScalarGridSpec(
        num_scalar_prefetch=2,
        grid=(nb, MLA_HEADS // hps, qi.shape[0]),
        in_specs=[
            pl.BlockSpec((None, tq, hps * 2 * LANES), lambda b, h, p, qi, ki: (b, qi[p], h)),
            pl.BlockSpec((None, tk, hps * 2 * LANES), lambda b, h, p, qi, ki: (b, ki[p], h)),
            pl.BlockSpec((None, tk, hps * LANES), lambda b, h, p, qi, ki: (b, ki[p], h)),
        ],
        out_specs=pl.BlockSpec((None, tq, hps * LANES), lambda b, h, p, qi, ki: (b, qi[p], h)),
        scratch_shapes=[pltpu.VMEM((hps, tq, 1), F32), pltpu.VMEM((hps, tq, 1), F32),
                        pltpu.VMEM((tq, hps * LANES), F32)],
    )
    return pl.pallas_call(
        functools.partial(_flash_kernel, tq=tq, tk=tk, hps=hps, tr=tr),
        out_shape=jax.ShapeDtypeStruct((nb, s, MLA_HEADS * LANES), BF16),
        grid_spec=grid_spec,
        compiler_params=_cparams(("arbitrary", "arbitrary", "arbitrary")),
        name="flash_mla",
    )(qi, ki, q, k, v)


def _sb_terms(z):
    tail = jnp.log(1.0 + jnp.exp(-jnp.abs(z)))
    return -(jnp.maximum(z, 0.0) + tail), jnp.minimum(z, 0.0) - tail


def _sb_kernel(qi_ref, ki_ref, q_ref, k_ref, v_ref, u_ref, o_ref, keep_sc, acc_sc, *, tq, tk, hps, tr):
    p = pl.program_id(2)
    qi = qi_ref[p]
    ki = ki_ref[p]

    @pl.when(ki == ((qi + 1) * tq - 1) // tk)
    def _():
        keep_sc[...] = jnp.zeros(keep_sc.shape, F32)
        acc_sc[...] = jnp.zeros(acc_sc.shape, F32)

    def update(masked):
        for hh in range(hps):
            hc = slice(SB_HEAD * hh, SB_HEAD * (hh + 1))
            k = k_ref[:, hc].astype(BF16)
            v = v_ref[:, hc].astype(BF16)
            for r0 in range(0, tq, tr):
                rows = slice(r0, r0 + tr)
                q = (q_ref[rows, hc] * SB_SCALE).astype(BF16)
                log_keep, log_break = _sb_terms(_dot_nt(q, k))
                if masked:
                    row = qi * tq + r0 + lax.broadcasted_iota(jnp.int32, (tr, tk), 0)
                    col = ki * tk + lax.broadcasted_iota(jnp.int32, (tr, tk), 1)
                    mask = row > col
                    log_keep = jnp.where(mask, log_keep, 0.0)
                a = jnp.exp(log_break + _dot01_right(log_keep, u_ref[...]) + keep_sc[hh, rows])
                if masked:
                    a = jnp.where(mask, a, 0.0)
                acc_sc[rows, hc] += _dot(a.astype(BF16), v)
                keep_sc[hh, rows] += jnp.sum(log_keep, axis=-1, keepdims=True)

    on_diagonal = (ki + 1) * tk > qi * tq

    @pl.when(on_diagonal)
    def _():
        update(True)

    @pl.when(jnp.logical_not(on_diagonal))
    def _():
        update(False)

    @pl.when(ki == 0)
    def _():
        o_ref[...] = acc_sc[...].astype(o_ref.dtype)


def _later_matrix(n):
    return (lax.broadcasted_iota(jnp.int32, (n, n), 0) > lax.broadcasted_iota(jnp.int32, (n, n), 1)).astype(BF16)


def sb_attention(q, kv, tq, tk, hps, tr):
    nb, s, _ = q.shape
    tq, tk = min(tq, s), min(tk, s)
    tr = min(tr, tq)
    assert SB_HEADS % hps == 0 and tq % tr == 0
    n_groups = SB_HEADS // hps
    qi, ki = _pairs(s // tq, tq, tk, newest_first=True)
    grid_spec = pltpu.PrefetchScalarGridSpec(
        num_scalar_prefetch=2,
        grid=(nb, n_groups, qi.shape[0]),
        in_specs=[
            pl.BlockSpec((None, tq, hps * SB_HEAD), lambda b, h, p, qi, ki: (b, qi[p], h)),
            pl.BlockSpec((None, tk, hps * SB_HEAD), lambda b, h, p, qi, ki: (b, ki[p], h)),
            pl.BlockSpec((None, tk, hps * SB_HEAD), lambda b, h, p, qi, ki: (b, ki[p], n_groups + h)),
            pl.BlockSpec((tk, tk), lambda b, h, p, qi, ki: (0, 0)),
        ],
        out_specs=pl.BlockSpec((None, tq, hps * SB_HEAD), lambda b, h, p, qi, ki: (b, qi[p], h)),
        scratch_shapes=[pltpu.VMEM((hps, tq, 1), F32), pltpu.VMEM((tq, hps * SB_HEAD), F32)],
    )
    return pl.pallas_call(
        functools.partial(_sb_kernel, tq=tq, tk=tk, hps=hps, tr=tr),
        out_shape=jax.ShapeDtypeStruct((nb, s, SB_HEADS * SB_HEAD), BF16),
        grid_spec=grid_spec,
        compiler_params=_cparams(("arbitrary", "arbitrary", "arbitrary")),
        name="sb_attention",
    )(qi, ki, q, kv, kv, _later_matrix(tk))


def _hgrn_kernel(x_ref, lbl_ref, st0_ref, go_ref, mc_ref, ml_ref, o_ref, stf_ref,
                 st_sc, qd_sc, kd_sc, kl_sc, el_sc, *, layer, chunk, t_valid, ts):
    s = pl.program_id(1)
    hk = HG_HEADS * HG_DK
    cdt = BF16 if chunk % 16 == 0 else F32

    @pl.when(s == 0)
    def _():
        for hd in range(HG_HEADS):
            st_sc[hd] = st0_ref[hd].T

    logits = lbl_ref[...]
    e = jnp.exp(logits - jnp.max(logits, axis=0, keepdims=True))
    sm = e / jnp.sum(e, axis=0, keepdims=True)
    lb = jnp.sum(sm[1:layer + 1], axis=0, keepdims=True)
    q = x_ref[:, :hk]
    forget = lb + (1.0 - lb) * jax.nn.sigmoid(x_ref[:, hk:2 * hk])
    k = 1.0 - forget
    log_f = jnp.log(forget)
    if t_valid < ts:
        valid = lax.broadcasted_iota(jnp.int32, (ts, hk), 0) < t_valid
        log_f = jnp.where(valid, log_f, 0.0)
        k = jnp.where(valid, k, 0.0)
    if cdt == BF16:
        cum = _dot01(mc_ref[...], log_f)
        last = _dot01(ml_ref[...], log_f)
    else:
        cum = jnp.dot(mc_ref[...], log_f, precision=lax.Precision.HIGHEST, preferred_element_type=F32)
        last = jnp.dot(ml_ref[...], log_f, precision=lax.Precision.HIGHEST, preferred_element_type=F32)
    qd_sc[...] = q * jnp.exp(cum)
    kd_sc[...] = k * jnp.exp(-cum)
    kl_sc[...] = k * jnp.exp(last - cum)
    el_sc[...] = jnp.exp(last)
    go = go_ref[...]
    causal = (lax.broadcasted_iota(jnp.int32, (chunk, chunk), 0)
              >= lax.broadcasted_iota(jnp.int32, (chunk, chunk), 1))

    hv = HG_HEADS * HG_DV
    for hd in range(HG_HEADS):
        cols = slice(HG_DK * hd, HG_DK * (hd + 1))
        vcols = slice(2 * hk + HG_DV * hd, 2 * hk + HG_DV * (hd + 1))
        gcols = slice(2 * hk + hv + HG_DV * hd, 2 * hk + hv + HG_DV * (hd + 1))
        st = st_sc[hd]
        for c in range(ts // chunk):
            rows = slice(c * chunk, (c + 1) * chunk)
            qd = qd_sc[rows, cols].astype(cdt)
            kd = kd_sc[rows, cols].astype(cdt)
            kl = kl_sc[rows, cols].astype(cdt)
            v = x_ref[rows, vcols].astype(cdt)
            el = el_sc[c * chunk:c * chunk + 1, cols]
            att = jnp.where(causal, _dot_nt(qd, kd), 0.0)
            o = _dot_nt(qd, st.astype(cdt)) + _dot(att.astype(cdt), v)
            st = st * el + _dot_tn(v, kl)
            on = _rms(o, go) * _silu(x_ref[rows, gcols])
            o_ref[rows, HG_DV * hd:HG_DV * (hd + 1)] = on.astype(o_ref.dtype)
        st_sc[hd] = st

    @pl.when(s == pl.num_programs(1) - 1)
    def _():
        for hd in range(HG_HEADS):
            stf_ref[hd] = st_sc[hd].T


def _chunk_matrices(ts, chunk):
    r = lax.broadcasted_iota(jnp.int32, (ts, ts), 0)
    c = lax.broadcasted_iota(jnp.int32, (ts, ts), 1)
    same = (r // chunk) == (c // chunk)
    dtype = BF16 if chunk % 16 == 0 else F32
    return (same & (c <= r)).astype(dtype), same.astype(dtype)


def hgrn_scan(x, lb_logits, state0, g_o, layer, chunk, t_valid, ts, out_dtype):
    nb, s, _ = x.shape
    ts = min(ts, s)
    hk, hv = HG_HEADS * HG_DK, HG_HEADS * HG_DV
    mc, ml = _chunk_matrices(ts, chunk)
    return pl.pallas_call(
        functools.partial(_hgrn_kernel, layer=layer, chunk=chunk, t_valid=t_valid, ts=ts),
        out_shape=[
            jax.ShapeDtypeStruct((nb, s, hv), out_dtype),
            jax.ShapeDtypeStruct((nb, HG_HEADS, HG_DK, HG_DV), F32),
        ],
        grid=(nb, s // ts),
        in_specs=[
            pl.BlockSpec((None, ts, 2 * hk + 2 * hv), lambda b, i: (b, i, 0)),
            pl.BlockSpec(lb_logits.shape, lambda b, i: (0, 0)),
            pl.BlockSpec((None, HG_HEADS, HG_DK, HG_DV), lambda b, i: (b, 0, 0, 0)),
            pl.BlockSpec((1, HG_DV), lambda b, i: (0, 0)),
            pl.BlockSpec((ts, ts), lambda b, i: (0, 0)),
            pl.BlockSpec((ts, ts), lambda b, i: (0, 0)),
        ],
        out_specs=[
            pl.BlockSpec((None, ts, hv), lambda b, i: (b, i, 0)),
            pl.BlockSpec((None, HG_HEADS, HG_DK, HG_DV), lambda b, i: (b, 0, 0, 0)),
        ],
        scratch_shapes=[
            pltpu.VMEM((HG_HEADS, HG_DV, HG_DK), F32),
            pltpu.VMEM((ts, hk), F32),
            pltpu.VMEM((ts, hk), F32),
            pltpu.VMEM((ts, hk), F32),
            pltpu.VMEM((ts, hk), F32),
        ],
        compiler_params=_cparams(("arbitrary", "arbitrary")),
        name="hgrn_scan",
    )(x, lb_logits, state0, g_o.reshape(1, HG_DV), mc, ml)


def _ffn_kernel(*refs, tm, seg, stream):
    if stream:
        (x_ref, g_ref, sh_ref, sc_ref, gate_ref, wa0_ref, wan_ref, wu_ref, cw_ref, cb_ref, wd_ref,
         y_ref, tail_ref, h_sc, acc_sc, a_sc, carry_sc) = refs
    else:
        (x_ref, g_ref, sh_ref, sc_ref, gate_ref, wa0_ref, wan_ref, wu_ref, cw_ref, cb_ref, wd_ref, past_ref,
         y_ref, tail_ref, h_sc, acc_sc, a_sc) = refs
    i = pl.program_id(1)
    f = pl.program_id(2)

    @pl.when(f == 0)
    def _():
        h_sc[...] = _norm_mod(x_ref[...], g_ref[...], sh_ref[...], sc_ref[...]).astype(BF16)
        acc_sc[...] = jnp.zeros(acc_sc.shape, F32)
        a_sc[0] = _dot(h_sc[...], wa0_ref[...])

    if stream:
        @pl.when(i == 0)
        def _():
            carry_sc[f] = jnp.zeros(carry_sc.shape[1:], F32)

    def step(cur, nxt):
        h = h_sc[...]
        a = a_sc[cur]
        a_sc[nxt] = _dot(h, wan_ref[...])
        u = _dot(h, wu_ref[...])
        tf = a.shape[1]
        row = lax.broadcasted_iota(jnp.int32, (tm, tf), 0)
        a1 = pltpu.roll(a, 1, 0)
        a2 = pltpu.roll(a, 2, 0)
        if stream:
            c = carry_sc[f]
            a1 = jnp.where(row == 0, c[SUBLANES - 1:SUBLANES], a1)
            a2 = jnp.where(row == 0, c[SUBLANES - 2:SUBLANES - 1],
                           jnp.where(row == 1, c[SUBLANES - 1:SUBLANES], a2))
            carry_sc[f] = a[tm - SUBLANES:, :]
            tail_ref[...] = a[tm - SUBLANES:, :]
        else:
            t = jnp.bitwise_and(row, seg - 1)
            a1 = jnp.where(t == 0, past_ref[1], a1)
            a2 = jnp.where(t < 2, past_ref[0], a2)
            tail_ref[...] = a
        cw = cw_ref[...]
        conv = cb_ref[...] + a2 * cw[0:1] + a1 * cw[1:2] + a * cw[2:3]
        mid = (_silu(conv) * u).astype(BF16)
        acc_sc[...] += _dot(mid, wd_ref[...])

    parity = jnp.bitwise_and(f, 1)

    @pl.when(parity == 0)
    def _():
        step(0, 1)

    @pl.when(parity == 1)
    def _():
        step(1, 0)

    @pl.when(f == pl.num_programs(2) - 1)
    def _():
        y_ref[...] = x_ref[...] + gate_ref[...] * acc_sc[...]


def conv_ffn(x, g, shift, scale, gate, w_up, conv_w, conv_b, w_down, past, tm, tf):
    nb, s, d = x.shape
    dff = w_down.shape[0]
    tm = min(tm, s)
    nf = dff // tf
    stream = past is None
    in_specs = [
        pl.BlockSpec((None, tm, d), lambda b, i, f: (b, i, 0)),
        pl.BlockSpec((1, d), lambda b, i, f: (0, 0)),
        _mod_spec(shift, tm),
        _mod_spec(scale, tm),
        _mod_spec(gate, tm),
        pl.BlockSpec((d, tf), lambda b, i, f: (0, 0)),
        pl.BlockSpec((d, tf), lambda b, i, f: (0, jnp.minimum(f + 1, nf - 1))),
        pl.BlockSpec((d, tf), lambda b, i, f: (0, nf + f)),
        pl.BlockSpec((CONV_W, tf), lambda b, i, f: (0, f)),
        pl.BlockSpec((1, tf), lambda b, i, f: (0, f)),
        pl.BlockSpec((tf, d), lambda b, i, f: (f, 0)),
    ]
    args = [x, g.reshape(1, d), shift, scale, gate, w_up, w_up, w_up, conv_w, conv_b.reshape(1, dff), w_down]
    scratch = [pltpu.VMEM((tm, d), BF16), pltpu.VMEM((tm, d), F32), pltpu.VMEM((2, tm, tf), F32)]
    if stream:
        tail_shape = jax.ShapeDtypeStruct((nb, s // tm, SUBLANES, dff), F32)
        tail_spec = pl.BlockSpec((None, None, SUBLANES, tf), lambda b, i, f: (b, i, 0, f))
        scratch.append(pltpu.VMEM((nf, SUBLANES, tf), F32))
    else:
        in_specs.append(pl.BlockSpec((CONV_W - 1, tm, tf), lambda b, i, f: (0, i, f)))
        args.append(past)
        tail_shape = jax.ShapeDtypeStruct((nb, s, dff), F32)
        tail_spec = pl.BlockSpec((None, tm, tf), lambda b, i, f: (b, i, f))
    return pl.pallas_call(
        functools.partial(_ffn_kernel, tm=tm, seg=T_PAD, stream=stream),
        out_shape=[jax.ShapeDtypeStruct((nb, s, d), F32), tail_shape],
        grid=(nb, s // tm, nf),
        in_specs=in_specs,
        out_specs=[pl.BlockSpec((None, tm, d), lambda b, i, f: (b, i, 0)), tail_spec],
        scratch_shapes=scratch,
        compiler_params=_cparams(("arbitrary", "arbitrary", "arbitrary")),
        name="conv_ffn",
    )(*args)


def _absorb_kernel(q_ref, wk_ref, gk_ref, qa_ref, qp_ref):
    gk = gk_ref[...]
    for hd in range(MLA_HEADS):
        c0 = 2 * LANES * hd
        qn = (q_ref[:, c0:c0 + LANES] * gk[:, :LANES]).astype(BF16)
        qa_ref[:, c0:c0 + 2 * LANES] = _dot_nt(qn, wk_ref[:, LANES * hd:LANES * (hd + 1)]).astype(qa_ref.dtype)
        qp_ref[:, LANES * hd:LANES * (hd + 1)] = (q_ref[:, c0 + LANES:c0 + 2 * LANES] * gk[:, LANES:]).astype(qp_ref.dtype)


def mla_absorb(q, wk, gk):
    m = q.shape[0]
    return pl.pallas_call(
        _absorb_kernel,
        out_shape=[jax.ShapeDtypeStruct((m, MLA_HEADS * KV_LORA), BF16),
                   jax.ShapeDtypeStruct((m, MLA_HEADS * LANES), BF16)],
        name="mla_absorb",
    )(q, wk, gk)


def _vup_kernel(o_ref, wv_ref, out_ref):
    for hd in range(MLA_HEADS):
        lat = o_ref[:, KV_LORA * hd:KV_LORA * (hd + 1)].astype(BF16)
        out_ref[:, V_HEAD * hd:V_HEAD * (hd + 1)] = _dot(lat, wv_ref[:, V_HEAD * hd:V_HEAD * (hd + 1)])


def mla_vup(o_lat, wv):
    m = o_lat.shape[0]
    return pl.pallas_call(
        _vup_kernel,
        out_shape=jax.ShapeDtypeStruct((m, MLA_HEADS * V_HEAD), F32),
        name="mla_vup",
    )(o_lat, wv)


def _mla_dec_kernel(pt_ref, new_ref, *refs, n_slots):
    page_refs = refs[:n_slots]
    qa_ref, qp_ref, wkt_ref, o_ref, m_sc, l_sc, acc_sc = refs[n_slots:]
    s = pl.program_id(1)
    n_rows = T_PAD * MLA_HEADS

    def scores(tile):
        cb = tile[:KV_LORA].astype(BF16)
        kpt = tile[KV_LORA:]
        kt = _dot(wkt_ref[...], cb)
        ss_pe = jnp.sum(kpt * kpt, axis=0, keepdims=True)
        ss = jnp.concatenate(
            [jnp.sum(kt[QK_NOPE * hd:QK_NOPE * (hd + 1)] ** 2, axis=0, keepdims=True) for hd in range(MLA_HEADS)],
            axis=0)
        r = lax.rsqrt((ss + ss_pe) * (1.0 / QK_HEAD) + EPS)
        sc = _dot(qa_ref[...], cb) + _dot(qp_ref[...], kpt.astype(BF16))
        return sc * jnp.concatenate([r] * T_PAD, axis=0), cb

    def update(scs, cbs):
        m_prev = m_sc[...]
        m_new = m_prev
        for sc in scs:
            m_new = jnp.maximum(m_new, jnp.max(sc, axis=-1, keepdims=True))
        alpha = jnp.exp(m_prev - m_new)
        l_new = alpha * l_sc[...]
        acc = alpha * acc_sc[...]
        for sc, cb in zip(scs, cbs):
            pr = jnp.exp(sc - m_new)
            l_new = l_new + jnp.sum(pr, axis=-1, keepdims=True)
            acc = acc + _dot_nt(pr.astype(BF16), cb)
        l_sc[...] = l_new
        acc_sc[...] = acc
        m_sc[...] = m_new

    @pl.when(s == 0)
    def _():
        m_sc[...] = jnp.full((n_rows, 1), NEG_BIG, F32)
        l_sc[...] = jnp.zeros((n_rows, 1), F32)
        acc_sc[...] = jnp.zeros((n_rows, KV_LORA), F32)
        sc, cb = scores(new_ref[...])
        tok = lax.shift_right_logical(lax.broadcasted_iota(jnp.int32, sc.shape, 0), HEAD_SHIFT)
        key = lax.broadcasted_iota(jnp.int32, sc.shape, 1)
        update([jnp.where(key <= tok, sc, NEG_BIG)], [cb])

    @pl.when(s > 0)
    def _():
        pairs = [scores(jnp.concatenate([page_refs[a][...], page_refs[a + 1][...]], axis=1))
                 for a in range(0, n_slots, 2)]
        update([p[0] for p in pairs], [p[1] for p in pairs])

    @pl.when(s == pl.num_programs(1) - 1)
    def _():
        o_ref[...] = acc_sc[...] / l_sc[...]


def mla_decode(page_table, new_page, cache_t, layer, qa, qp, wk_t, n_slots):
    db, n_pages = page_table.shape
    n_rows = T_PAD * MLA_HEADS
    row_w = cache_t.shape[2]
    assert n_slots % 2 == 0 and n_pages % n_slots == 0
    steps = n_pages // n_slots

    def page_spec(slot):
        def imap(b, s, pt):
            idx = jnp.maximum(s - 1, 0) * n_slots + slot
            return (layer, pt[b * n_pages + idx], 0, 0)
        return pl.BlockSpec((None, None, row_w, PAGE_SIZE), imap)

    grid_spec = pltpu.PrefetchScalarGridSpec(
        num_scalar_prefetch=1,
        grid=(db, steps + 1),
        in_specs=[pl.BlockSpec((None, row_w, PAGE_SIZE), lambda b, s, pt: (b, 0, 0))]
        + [page_spec(i) for i in range(n_slots)]
        + [
            pl.BlockSpec((None, n_rows, KV_LORA), lambda b, s, pt: (b, 0, 0)),
            pl.BlockSpec((None, n_rows, QK_ROPE), lambda b, s, pt: (b, 0, 0)),
            pl.BlockSpec(wk_t.shape, lambda b, s, pt: (0, 0)),
        ],
        out_specs=pl.BlockSpec((None, n_rows, KV_LORA), lambda b, s, pt: (b, 0, 0)),
        scratch_shapes=[pltpu.VMEM((n_rows, 1), F32), pltpu.VMEM((n_rows, 1), F32),
                        pltpu.VMEM((n_rows, KV_LORA), F32)],
    )
    return pl.pallas_call(
        functools.partial(_mla_dec_kernel, n_slots=n_slots),
        out_shape=jax.ShapeDtypeStruct((db, n_rows, KV_LORA), F32),
        grid_spec=grid_spec,
        compiler_params=_cparams(("arbitrary", "arbitrary")),
        name="mla_decode",
    )(page_table.reshape(-1), new_page, *([cache_t] * n_slots), qa, qp, wk_t)


def _sb_dec_kernel(pt_ref, new_ref, *refs, n_slots):
    page_refs = refs[:n_slots]
    q_ref, u_ref, un_ref, shn_ref, o_ref, keep_sc, acc_sc = refs[n_slots:]
    s = pl.program_id(1)
    n_rows = T_PAD * SB_HEADS

    def process(pages, n_keys, causal, cdt, keep, acc, later_of, shift_of):
        w = 2 * n_keys
        qs = [(q_ref[:, SB_HEAD * hd:SB_HEAD * (hd + 1)] * SB_SCALE).astype(cdt) for hd in range(SB_HEADS)]
        kvs = [[ref[pl.ds(hd, w, stride=SB_HEADS), :].astype(cdt) for hd in range(SB_HEADS)] for ref in pages]
        z = jnp.concatenate(
            [jnp.concatenate([_dot_nt(qs[hd], kv[hd]) for hd in range(SB_HEADS)], axis=0) for kv in kvs],
            axis=1)
        log_keep, log_break = _sb_terms(z)
        lane = lax.broadcasted_iota(jnp.int32, z.shape, 1)
        mask = jnp.bitwise_and(lane, 1) == 0
        if causal:
            tok = jnp.bitwise_and(lax.broadcasted_iota(jnp.int32, z.shape, 0), T_PAD - 1)
            mask = mask & (lax.shift_right_logical(lane, 1) < tok)
        log_keep = jnp.where(mask, log_keep, 0.0)
        later = []
        for p in range(len(pages)):
            lk = log_keep[:, w * p:w * (p + 1)]
            later.append(later_of(lk) + keep)
            keep = keep + jnp.sum(lk, axis=-1, keepdims=True)
        a = jnp.where(mask, jnp.exp(log_break + jnp.concatenate(later, axis=1)), 0.0)
        a = shift_of(a)
        outs = []
        for hd in range(SB_HEADS):
            rows = slice(T_PAD * hd, T_PAD * (hd + 1))
            o = None
            for p, kv in enumerate(kvs):
                d = _dot(a[rows, w * p:w * (p + 1)].astype(cdt), kv[hd])
                o = d if o is None else o + d
            outs.append(o)
        return keep, acc + jnp.concatenate(outs, axis=0)

    @pl.when(s == 0)
    def _():
        exact = functools.partial(jnp.dot, precision=lax.Precision.HIGHEST, preferred_element_type=F32)
        keep, acc = process([new_ref], T_PAD, True, F32, jnp.zeros((n_rows, 1), F32),
                            jnp.zeros((n_rows, SB_HEAD), F32),
                            lambda lk: exact(lk, un_ref[...]), lambda a: exact(a, shn_ref[...]))
        keep_sc[...] = keep
        acc_sc[...] = acc

    @pl.when(s > 0)
    def _():
        keep, acc = process(page_refs, PAGE_SIZE, False, BF16, keep_sc[...], acc_sc[...],
                            lambda lk: _dot01_right(lk, u_ref[...]),
                            lambda a: pltpu.roll(a, 1, 1))
        keep_sc[...] = keep
        acc_sc[...] = acc

    @pl.when(s == pl.num_programs(1) - 1)
    def _():
        for hd in range(SB_HEADS):
            o_ref[:, SB_HEAD * hd:SB_HEAD * (hd + 1)] = acc_sc[T_PAD * hd:T_PAD * (hd + 1), :]


def sb_decode(page_table, new_page, cache_rows, layer, q, n_slots):
    db, n_pages = page_table.shape
    n_rows = T_PAD * SB_HEADS
    hw = SB_HEADS * SB_HEAD
    page_rows = cache_rows.shape[2]
    assert n_pages % n_slots == 0
    steps = n_pages // n_slots

    def page_spec(slot):
        def imap(b, s, pt):
            idx = n_pages - 1 - (jnp.maximum(s - 1, 0) * n_slots + slot)
            return (layer, pt[b * n_pages + idx], 0, 0)
        return pl.BlockSpec((None, None, page_rows, SB_HEAD), imap)

    grid_spec = pltpu.PrefetchScalarGridSpec(
        num_scalar_prefetch=1,
        grid=(db, steps + 1),
        in_specs=[pl.BlockSpec((None,) + new_page.shape[1:], lambda b, s, pt: (b, 0, 0))]
        + [page_spec(i) for i in range(n_slots)]
        + [pl.BlockSpec((None, T_PAD, hw), lambda b, s, pt: (b, 0, 0))]
        + [pl.BlockSpec((2 * n, 2 * n), lambda b, s, pt: (0, 0)) for n in (PAGE_SIZE, T_PAD, T_PAD)],
        out_specs=pl.BlockSpec((None, T_PAD, hw), lambda b, s, pt: (b, 0, 0)),
        scratch_shapes=[pltpu.VMEM((n_rows, 1), F32), pltpu.VMEM((n_rows, SB_HEAD), F32)],
    )
    return pl.pallas_call(
        functools.partial(_sb_dec_kernel, n_slots=n_slots),
        out_shape=jax.ShapeDtypeStruct((db, T_PAD, hw), F32),
        grid_spec=grid_spec,
        compiler_params=_cparams(("arbitrary", "arbitrary")),
        name="sb_decode",
    )(page_table.reshape(-1), new_page, *([cache_rows] * n_slots), q,
      _interleaved_matrices(PAGE_SIZE, BF16)[0], *_interleaved_matrices(T_PAD, F32))


def _interleaved_matrices(n_keys, dtype):
    r = lax.broadcasted_iota(jnp.int32, (2 * n_keys, 2 * n_keys), 0)
    c = lax.broadcasted_iota(jnp.int32, (2 * n_keys, 2 * n_keys), 1)
    return ((r // 2) > (c // 2)).astype(dtype), (c == r + 1).astype(dtype)


TM_LINEAR = 512
TM_FFN = 1024
TF_FFN = 256
TQ_MLA, TK_MLA = 512, 512
TQ_SB, TK_SB = 512, 256
MLA_HEADS_PER_STEP, MLA_ROW_BLOCK = 4, 256
SB_HEADS_PER_STEP, SB_ROW_BLOCK = 4, 512
TS_HGRN = 256
MLA_PAGES_PER_STEP = 16
SB_PAGES_PER_STEP = 8


def _pad_tokens(x, t_pad):
    return jnp.pad(x, ((0, 0), (0, t_pad - x.shape[1])) + ((0, 0),) * (x.ndim - 2))


def kernel(x_prompt, x_sample, c_prompt, c_sample, page_table, cache_mla, cache_sb_kv, state_hgrn,
           state_ffn_conv, w_mod, b_mod, norm_g, w_mla_a, g_mla_qa, g_mla_kva, w_mla_uq, w_mla_ukv,
           g_mla_q, g_mla_k, w_mla_o, w_sb_qkv, w_sb_o, w_hg_in, hg_lb_logits, g_hg_o, w_hg_o,
           w_ffn_up, ffn_conv_w, ffn_conv_b, w_ffn_down):
    b, s_len, d = x_prompt.shape
    db, t, _ = x_sample.shape
    depth = w_mod.shape[0]
    n_pages = page_table.shape[1]
    past = n_pages * PAGE_SIZE
    dff = w_ffn_down.shape[1]
    assert t <= T_PAD and t >= CONV_W - 1

    mod = ada_mod(jnp.concatenate([c_prompt, c_sample], axis=0), w_mod, b_mod)
    mod = mod.reshape(depth, b + db, N_MOD, d)
    cos_p, sin_p = _rope_tables(jnp.arange(s_len))
    cos_s, sin_s = _rope_tables(jnp.tile(past + jnp.arange(T_PAD), db))
    sb_cache = cache_sb_kv.reshape(cache_sb_kv.shape[:2] + (PAGE_SIZE * 2 * SB_HEADS, SB_HEAD))
    mla_cache_t = jnp.swapaxes(cache_mla, 2, 3)

    xp = x_prompt
    xs = _pad_tokens(x_sample, T_PAD).reshape(1, db * T_PAD, d)
    mla_p, mla_s, sb_p, sb_s, hg_p, hg_s, ffn_p, ffn_s = [], [], [], [], [], [], [], []
    for i in range(depth):
        mp = [mod[i, :b, j].reshape(b, 1, d) for j in range(N_MOD)]
        ms = [jnp.repeat(mod[i, b:, j], T_PAD, axis=0).reshape(1, db * T_PAD, d) for j in range(N_MOD)]
        kind, j = i % N_MIXERS, i // N_MIXERS
        if kind == 0:
            wp = _mla_weights(w_mla_a[j], g_mla_qa[j], g_mla_kva[j], w_mla_uq[j], w_mla_ukv[j],
                              g_mla_q[j], g_mla_k[j])
            rows_p, qp, kp, vp = mla_proj(xp, norm_g[i, 0], mp[0], mp[1], cos_p, sin_p, wp, BF16, TM_LINEAR)
            op = flash_mla(qp, kp, vp, TQ_MLA, TK_MLA, MLA_HEADS_PER_STEP, MLA_ROW_BLOCK)
            rows_s, qs, _, _ = mla_proj(xs, norm_g[i, 0], ms[0], ms[1], cos_s, sin_s, wp, F32, TM_LINEAR)
            qa, qpe = mla_absorb(qs[0], wp["wk"], wp["gk"])
            qa = qa.reshape(db, T_PAD * MLA_HEADS, KV_LORA)
            qpe = qpe.reshape(db, T_PAD * MLA_HEADS, LANES)[:, :, :QK_ROPE]
            rows_s = rows_s.reshape(db, T_PAD, -1)
            new_page = jnp.swapaxes(_pad_tokens(rows_s, PAGE_SIZE), 1, 2)
            o_lat = mla_decode(page_table, new_page, mla_cache_t, j, qa, qpe, wp["wk_t"], MLA_PAGES_PER_STEP)
            os_ = mla_vup(o_lat.reshape(db * T_PAD, MLA_HEADS * KV_LORA), wp["wv"])
            os_ = os_.reshape(1, db * T_PAD, -1)
            w_o = w_mla_o[j]
            mla_p.append(rows_p)
            mla_s.append(rows_s[:, :t])
        elif kind == 1:
            wq = w_sb_qkv[j].astype(BF16)
            hw = SB_HEADS * SB_HEAD
            q_p, kv_p = norm_mod_linear(xp, norm_g[i, 0], mp[0], mp[1], [wq[:, :hw], wq[:, hw:]], TM_LINEAR)
            op = sb_attention(q_p, kv_p, TQ_SB, TK_SB, SB_HEADS_PER_STEP, SB_ROW_BLOCK)
            q_s, kv_s = norm_mod_linear(xs, norm_g[i, 0], ms[0], ms[1], [wq[:, :hw], wq[:, hw:]], TM_LINEAR)
            kv_s = kv_s.reshape(db, T_PAD, 2 * hw)
            new_page = kv_s.reshape(db, T_PAD * 2 * SB_HEADS, SB_HEAD)
            os_ = sb_decode(page_table, new_page, sb_cache, j, q_s.reshape(db, T_PAD, hw), SB_PAGES_PER_STEP)
            os_ = os_.reshape(1, db * T_PAD, hw)
            w_o = w_sb_o[j]
            sb_p.append(kv_p.reshape(b, s_len, 2, SB_HEADS, SB_HEAD))
            sb_s.append(kv_s[:, :t].reshape(db, t, 2, SB_HEADS, SB_HEAD))
        else:
            w_in = w_hg_in[j].astype(BF16)
            (zp,) = norm_mod_linear(xp, norm_g[i, 0], mp[0], mp[1], [w_in], TM_LINEAR)
            zero_state = jnp.zeros((b, HG_HEADS, HG_DK, HG_DV), F32)
            op, st_p = hgrn_scan(zp, hg_lb_logits, zero_state, g_hg_o[j], i, HG_CHUNK, s_len, TS_HGRN, BF16)
            (zs,) = norm_mod_linear(xs, norm_g[i, 0], ms[0], ms[1], [w_in], TM_LINEAR)
            os_, st_s = hgrn_scan(zs.reshape(db, T_PAD, -1), hg_lb_logits, state_hgrn[j], g_hg_o[j], i,
                                  T_PAD, t, T_PAD, F32)
            os_ = os_.reshape(1, db * T_PAD, -1)
            w_o = w_hg_o[j]
            hg_p.append(st_p)
            hg_s.append(st_s)
        w_o = w_o.astype(BF16)
        xp = proj_residual(op, w_o, xp, mp[2], TM_LINEAR)
        xs = proj_residual(os_, w_o, xs, ms[2], TM_LINEAR)

        w_up = w_ffn_up[i].astype(BF16)
        w_down = w_ffn_down[i].astype(BF16)
        xp, tail_p = conv_ffn(xp, norm_g[i, 1], mp[3], mp[4], mp[5], w_up, ffn_conv_w[i], ffn_conv_b[i],
                              w_down, None, TM_FFN, TF_FFN)
        st = state_ffn_conv[i]
        z = jnp.zeros((db, T_PAD - 2, dff), F32)
        past2 = jnp.concatenate([st, z], axis=1)
        past1 = jnp.concatenate([st[:, 1:], z, z[:, :1]], axis=1)
        past_rows = jnp.stack([past2, past1]).reshape(CONV_W - 1, db * T_PAD, dff)
        xs, a_s = conv_ffn(xs, norm_g[i, 1], ms[3], ms[4], ms[5], w_up, ffn_conv_w[i], ffn_conv_b[i],
                           w_down, past_rows, TM_FFN, TF_FFN)
        ffn_p.append(tail_p[:, -1, SUBLANES - (CONV_W - 1):])
        ffn_s.append(a_s.reshape(db, T_PAD, dff)[:, t - (CONV_W - 1):t])
    ys = xs.reshape(db, T_PAD, d)[:, :t]
    return (xp, ys, jnp.stack(mla_p), jnp.stack(mla_s), jnp.stack(sb_p), jnp.stack(sb_s),
            jnp.stack(hg_p), jnp.stack(hg_s), jnp.stack(ffn_p), jnp.stack(ffn_s))
```

```python
import functools

import jax
import jax.numpy as jnp
import numpy as np
from jax import lax
from jax.experimental import pallas as pl
from jax.experimental.pallas import tpu as pltpu

F32 = jnp.float32
BF16 = jnp.bfloat16

N_MIXERS = 3
MLA_HEADS = 8
QK_NOPE = 128
QK_ROPE = 64
V_HEAD = 128
Q_LORA = 256
KV_LORA = 256
QK_HEAD = QK_NOPE + QK_ROPE
MLA_SCALE = QK_HEAD ** -0.5
ROPE_BASE = 10000.0
SB_HEADS = 8
SB_HEAD = 128
SB_SCALE = SB_HEAD ** -0.5
HG_HEADS = 8
HG_DK = 128
HG_DV = 128
HG_CHUNK = 32
CONV_W = 3
PAGE_SIZE = 128
EPS = 1e-6
N_MOD = 6

LANES = 128
SUBLANES = 8
T_PAD = SUBLANES
NEG_BIG = -1e30
HEAD_SHIFT = 3
LANE_SHIFT = 7
VMEM_LIMIT = 48 * 1024 * 1024


def _cparams(sem):
    return pltpu.CompilerParams(dimension_semantics=sem, vmem_limit_bytes=VMEM_LIMIT)


def _dot(a, b):
    return jnp.dot(a, b, preferred_element_type=F32)


def _dot_nt(a, b):
    return lax.dot_general(a, b, (((1,), (1,)), ((), ())), preferred_element_type=F32)


def _dot_tn(a, b):
    return lax.dot_general(a, b, (((0,), (0,)), ((), ())), preferred_element_type=F32)


def _split_bf16(x, parts):
    out = []
    r = x
    for _ in range(parts):
        p = r.astype(BF16)
        out.append(p)
        r = r - p.astype(F32)
    return out


def _dot01(m01, x, parts=3):
    acc = None
    for p in _split_bf16(x, parts):
        d = _dot(m01, p)
        acc = d if acc is None else acc + d
    return acc


def _dot01_right(x, m01, parts=2):
    acc = None
    for p in _split_bf16(x, parts):
        d = _dot(p, m01)
        acc = d if acc is None else acc + d
    return acc


def _rms(x, g):
    ms = jnp.mean(x * x, axis=-1, keepdims=True)
    return x * lax.rsqrt(ms + EPS) * g


def _norm_mod(x, g, shift, scale):
    return _rms(x, g) * (1.0 + scale) + shift


def _silu(x):
    return x * jax.nn.sigmoid(x)


def _mod_kernel(c_ref, w_ref, b_ref, o_ref):
    s = _silu(c_ref[...])
    o_ref[...] = _dot(s.astype(BF16), w_ref[...].astype(BF16)) + b_ref[...]


def ada_mod(c_all, w_mod, b_mod):
    depth, d, n = w_mod.shape
    nc = c_all.shape[0]
    tn = n // 4
    return pl.pallas_call(
        _mod_kernel,
        out_shape=jax.ShapeDtypeStruct((depth, nc, n), F32),
        grid=(depth, n // tn),
        in_specs=[
            pl.BlockSpec((nc, d), lambda l, j: (0, 0)),
            pl.BlockSpec((None, d, tn), lambda l, j: (l, 0, j)),
            pl.BlockSpec((None, 1, tn), lambda l, j: (l, 0, j)),
        ],
        out_specs=pl.BlockSpec((None, nc, tn), lambda l, j: (l, 0, j)),
        compiler_params=_cparams(("arbitrary", "arbitrary")),
        name="ada_mod",
    )(c_all, w_mod, b_mod.reshape(depth, 1, n))


def _mod_spec(mod, tm):
    r = mod.shape[1]
    if r == 1:
        return pl.BlockSpec((None, 1, mod.shape[2]), lambda b, i, *_: (b, 0, 0))
    return pl.BlockSpec((None, tm, mod.shape[2]), lambda b, i, *_: (b, i, 0))


def _nml_kernel(x_ref, g_ref, sh_ref, sc_ref, *rest, n_w):
    h = _norm_mod(x_ref[...], g_ref[...], sh_ref[...], sc_ref[...]).astype(BF16)
    for w_ref, o_ref in zip(rest[:n_w], rest[n_w:]):
        o_ref[...] = _dot(h, w_ref[...]).astype(o_ref.dtype)


def norm_mod_linear(x, g, shift, scale, ws, tm):
    nb, s, d = x.shape
    tm = min(tm, s)
    in_specs = [
        pl.BlockSpec((None, tm, d), lambda b, i: (b, i, 0)),
        pl.BlockSpec((1, d), lambda b, i: (0, 0)),
        _mod_spec(shift, tm),
        _mod_spec(scale, tm),
    ]
    out_shape, out_specs = [], []
    for w in ws:
        in_specs.append(pl.BlockSpec(w.shape, lambda b, i: (0, 0)))
        out_shape.append(jax.ShapeDtypeStruct((nb, s, w.shape[1]), F32))
        out_specs.append(pl.BlockSpec((None, tm, w.shape[1]), lambda b, i: (b, i, 0)))
    return pl.pallas_call(
        functools.partial(_nml_kernel, n_w=len(ws)),
        out_shape=out_shape,
        grid=(nb, s // tm),
        in_specs=in_specs,
        out_specs=out_specs,
        compiler_params=_cparams(("arbitrary", "arbitrary")),
        name="norm_mod_linear",
    )(x, g.reshape(1, d), shift, scale, *ws)


def _mla_proj_kernel(x_ref, g_ref, sh_ref, sc_ref, cos_ref, sin_ref, wa_ref, gqa_ref, gkva_ref,
                     wqm_ref, wqs_ref, wkv_ref, gq_ref, gk_ref,
                     rows_ref, q_ref, k_ref, v_ref):
    h = _norm_mod(x_ref[...], g_ref[...], sh_ref[...], sc_ref[...]).astype(BF16)
    a = _dot(h, wa_ref[...])
    cos = cos_ref[...]
    sin = sin_ref[...]
    qn = _rms(a[:, :Q_LORA], gqa_ref[...]).astype(BF16)
    cn = _rms(a[:, Q_LORA:Q_LORA + KV_LORA], gkva_ref[...])
    o = Q_LORA + KV_LORA
    kpe = a[:, o:o + LANES] * cos + a[:, o + LANES:o + 2 * LANES] * sin
    rows_ref[...] = jnp.concatenate([cn, kpe[:, :QK_ROPE]], axis=-1)
    qm = _dot(qn, wqm_ref[...])
    qs = _dot(qn, wqs_ref[...])
    kv = _dot(cn.astype(BF16), wkv_ref[...])
    gq = gq_ref[...]
    gk = gk_ref[...]
    kpe_ss = jnp.sum(kpe * kpe, axis=-1, keepdims=True)
    inv = 1.0 / QK_HEAD
    for hd in range(MLA_HEADS):
        c0 = 2 * LANES * hd
        nope = qm[:, c0:c0 + LANES]
        pe = qm[:, c0 + LANES:c0 + 2 * LANES] * cos + qs[:, LANES * hd:LANES * (hd + 1)] * sin
        ss = jnp.sum(nope * nope, axis=-1, keepdims=True) + jnp.sum(pe * pe, axis=-1, keepdims=True)
        r = lax.rsqrt(ss * inv + EPS) * MLA_SCALE
        q_ref[:, c0:c0 + LANES] = (nope * r * gq[:, :LANES]).astype(q_ref.dtype)
        q_ref[:, c0 + LANES:c0 + 2 * LANES] = (pe * r * gq[:, LANES:]).astype(q_ref.dtype)
        kn = kv[:, c0:c0 + LANES]
        rk = lax.rsqrt((jnp.sum(kn * kn, axis=-1, keepdims=True) + kpe_ss) * inv + EPS)
        k_ref[:, c0:c0 + LANES] = (kn * rk * gk[:, :LANES]).astype(k_ref.dtype)
        k_ref[:, c0 + LANES:c0 + 2 * LANES] = (kpe * rk * gk[:, LANES:]).astype(k_ref.dtype)
        v_ref[:, LANES * hd:LANES * (hd + 1)] = kv[:, c0 + LANES:c0 + 2 * LANES].astype(v_ref.dtype)


def mla_proj(x, g, shift, scale, cos, sin, wp, q_dtype, tm):
    nb, s, d = x.shape
    tm = min(tm, s)
    hw = 2 * LANES * MLA_HEADS
    full = lambda arr: pl.BlockSpec(arr.shape, lambda b, i: (0, 0))
    return pl.pallas_call(
        _mla_proj_kernel,
        out_shape=[
            jax.ShapeDtypeStruct((nb, s, KV_LORA + QK_ROPE), F32),
            jax.ShapeDtypeStruct((nb, s, hw), q_dtype),
            jax.ShapeDtypeStruct((nb, s, hw), BF16),
            jax.ShapeDtypeStruct((nb, s, LANES * MLA_HEADS), BF16),
        ],
        grid=(nb, s // tm),
        in_specs=[
            pl.BlockSpec((None, tm, d), lambda b, i: (b, i, 0)),
            pl.BlockSpec((1, d), lambda b, i: (0, 0)),
            _mod_spec(shift, tm),
            _mod_spec(scale, tm),
            pl.BlockSpec((tm, LANES), lambda b, i: (i, 0)),
            pl.BlockSpec((tm, LANES), lambda b, i: (i, 0)),
            full(wp["wa"]), full(wp["gqa"]), full(wp["gkva"]), full(wp["wqm"]), full(wp["wqs"]),
            full(wp["wkv"]), full(wp["gq"]), full(wp["gk"]),
        ],
        out_specs=[
            pl.BlockSpec((None, tm, KV_LORA + QK_ROPE), lambda b, i: (b, i, 0)),
            pl.BlockSpec((None, tm, hw), lambda b, i: (b, i, 0)),
            pl.BlockSpec((None, tm, hw), lambda b, i: (b, i, 0)),
            pl.BlockSpec((None, tm, LANES * MLA_HEADS), lambda b, i: (b, i, 0)),
        ],
        compiler_params=_cparams(("arbitrary", "arbitrary")),
        name="mla_proj",
    )(x, g.reshape(1, d), shift, scale, cos, sin, wp["wa"], wp["gqa"], wp["gkva"], wp["wqm"], wp["wqs"],
      wp["wkv"], wp["gq"], wp["gk"])


def _mla_weights(w_a, g_qa, g_kva, w_uq, w_ukv, g_q, g_k):
    d = w_a.shape[0]
    half = QK_ROPE // 2
    zpad = jnp.zeros((d, LANES - QK_ROPE), F32)
    w_pe = w_a[:, Q_LORA + KV_LORA:]
    w_pe_sw = jnp.concatenate([w_pe[:, half:], w_pe[:, :half]], axis=1)
    wa = jnp.concatenate([w_a[:, :Q_LORA + KV_LORA], w_pe, zpad, w_pe_sw, zpad], axis=1).astype(BF16)
    wq = w_uq.reshape(Q_LORA, MLA_HEADS, QK_HEAD)
    wq_pe = wq[:, :, QK_NOPE:]
    z = jnp.zeros((Q_LORA, MLA_HEADS, LANES - QK_ROPE), F32)
    wqm = jnp.concatenate([wq, z], axis=2).reshape(Q_LORA, MLA_HEADS * 2 * LANES).astype(BF16)
    wq_sw = jnp.concatenate([wq_pe[:, :, half:], wq_pe[:, :, :half], z], axis=2)
    wqs = wq_sw.reshape(Q_LORA, MLA_HEADS * LANES).astype(BF16)
    gpad = jnp.zeros((2 * LANES - QK_HEAD,), F32)
    wkv3 = w_ukv.reshape(KV_LORA, MLA_HEADS, QK_NOPE + V_HEAD)
    return {
        "wa": wa,
        "gqa": g_qa.reshape(1, Q_LORA),
        "gkva": g_kva.reshape(1, KV_LORA),
        "wqm": wqm,
        "wqs": wqs,
        "wkv": w_ukv.astype(BF16),
        "gq": jnp.concatenate([g_q, gpad]).reshape(1, 2 * LANES),
        "gk": jnp.concatenate([g_k, gpad]).reshape(1, 2 * LANES),
        "wk_t": wkv3[:, :, :QK_NOPE].reshape(KV_LORA, MLA_HEADS * QK_NOPE).T.astype(BF16),
        "wk": wkv3[:, :, :QK_NOPE].reshape(KV_LORA, MLA_HEADS * QK_NOPE).astype(BF16),
        "wv": wkv3[:, :, QK_NOPE:].reshape(KV_LORA, MLA_HEADS * V_HEAD).astype(BF16),
    }


def _rope_tables(pos):
    half = QK_ROPE // 2
    inv = ROPE_BASE ** (-jnp.arange(half, dtype=F32) / half)
    ang = pos.astype(F32)[:, None] * inv[None, :]
    cos, sin = jnp.cos(ang), jnp.sin(ang)
    z = jnp.zeros((pos.shape[0], LANES - QK_ROPE), F32)
    return jnp.concatenate([cos, cos, z], axis=1), jnp.concatenate([-sin, sin, z], axis=1)


def _pairs(n_q, tq, tk, newest_first):
    qi, ki = [], []
    for a in range(n_q):
        last = ((a + 1) * tq - 1) // tk
        ks = range(last, -1, -1) if newest_first else range(last + 1)
        for b in ks:
            qi.append(a)
            ki.append(b)
    return jnp.asarray(qi, jnp.int32), jnp.asarray(ki, jnp.int32)


def _flash_kernel(qi_ref, ki_ref, q_ref, k_ref, v_ref, o_ref, m_sc, l_sc, acc_sc, *, tq, tk, hps, tr):
    p = pl.program_id(2)
    qi = qi_ref[p]
    ki = ki_ref[p]

    @pl.when(ki == 0)
    def _():
        m_sc[...] = jnp.full(m_sc.shape, NEG_BIG, F32)
        l_sc[...] = jnp.zeros(l_sc.shape, F32)
        acc_sc[...] = jnp.zeros(acc_sc.shape, F32)

    def update(masked):
        for hh in range(hps):
            qk = slice(2 * LANES * hh, 2 * LANES * (hh + 1))
            vc = slice(LANES * hh, LANES * (hh + 1))
            for r0 in range(0, tq, tr):
                rows = slice(r0, r0 + tr)
                s = _dot_nt(q_ref[rows, qk], k_ref[:, qk])
                if masked:
                    row = qi * tq + r0 + lax.broadcasted_iota(jnp.int32, (tr, tk), 0)
                    col = ki * tk + lax.broadcasted_iota(jnp.int32, (tr, tk), 1)
                    s = jnp.where(row >= col, s, NEG_BIG)
                m_prev = m_sc[hh, rows]
                m_new = jnp.maximum(m_prev, jnp.max(s, axis=-1, keepdims=True))
                alpha = jnp.exp(m_prev - m_new)
                pr = jnp.exp(s - m_new)
                l_sc[hh, rows] = alpha * l_sc[hh, rows] + jnp.sum(pr, axis=-1, keepdims=True)
                acc_sc[rows, vc] = alpha * acc_sc[rows, vc] + _dot(pr.astype(BF16), v_ref[:, vc])
                m_sc[hh, rows] = m_new

    on_diagonal = (ki + 1) * tk > qi * tq

    @pl.when(on_diagonal)
    def _():
        update(True)

    @pl.when(jnp.logical_not(on_diagonal))
    def _():
        update(False)

    @pl.when(ki == ((qi + 1) * tq - 1) // tk)
    def _():
        for hh in range(hps):
            vc = slice(LANES * hh, LANES * (hh + 1))
            o_ref[:, vc] = (acc_sc[:, vc] / l_sc[hh]).astype(o_ref.dtype)


def flash_mla(q, k, v, tq, tk, hps, tr):
    nb, s, _ = q.shape
    tq, tk = min(tq, s), min(tk, s)
    tr = min(tr, tq)
    assert MLA_HEADS % hps == 0 and tq % tr == 0
    qi, ki = _pairs(s // tq, tq, tk, newest_first=False)
    grid_spec = pltpu.PrefetchScalarGridSpec(
        num_scalar_prefetch=2,
        grid=(nb, MLA_HEADS // hps, qi.shape[0]),
        in_specs=[
            pl.BlockSpec((None, tq, hps * 2 * LANES), lambda b, h, p, qi, ki: (b, qi[p], h)),
            pl.BlockSpec((None, tk, hps * 2 * LANES), lambda b, h, p, qi, ki: (b, ki[p], h)),
            pl.BlockSpec((None, tk, hps * LANES), lambda b, h, p, qi, ki: (b, ki[p], h)),
        ],
        out_specs=pl.BlockSpec((None, tq, hps * LANES), lambda b, h, p, qi, ki: (b, qi[p], h)),
        scratch_shapes=[pltpu.VMEM((hps, tq, 1), F32), pltpu.VMEM((hps, tq, 1), F32),
                        pltpu.VMEM((tq, hps * LANES), F32)],
    )
    return pl.pallas_call(
        functools.partial(_flash_kernel, tq=tq, tk=tk, hps=hps, tr=tr),
        out_shape=jax.ShapeDtypeStruct((nb, s, MLA_HEADS * LANES), BF16),
        grid_spec=grid_spec,
        compiler_params=_cparams(("arbitrary", "arbitrary", "arbitrary")),
        name="flash_mla",
    )(qi, ki, q, k, v)


def _sb_terms(z):
    tail = jnp.log(1.0 + jnp.exp(-jnp.abs(z)))
    return -(jnp.maximum(z, 0.0) + tail), jnp.minimum(z, 0.0) - tail


def _sb_kernel(qi_ref, ki_ref, q_ref, k_ref, v_ref, u_ref, o_ref, keep_sc, acc_sc, *, tq, tk, hps, tr):
    p = pl.program_id(2)
    qi = qi_ref[p]
    ki = ki_ref[p]

    @pl.when(ki == ((qi + 1) * tq - 1) // tk)
    def _():
        keep_sc[...] = jnp.zeros(keep_sc.shape, F32)
        acc_sc[...] = jnp.zeros(acc_sc.shape, F32)

    def update(masked):
        for hh in range(hps):
            hc = slice(SB_HEAD * hh, SB_HEAD * (hh + 1))
            k = k_ref[:, hc].astype(BF16)
            v = v_ref[:, hc].astype(BF16)
            for r0 in range(0, tq, tr):
                rows = slice(r0, r0 + tr)
                q = (q_ref[rows, hc] * SB_SCALE).astype(BF16)
                log_keep, log_break = _sb_terms(_dot_nt(q, k))
                if masked:
                    row = qi * tq + r0 + lax.broadcasted_iota(jnp.int32, (tr, tk), 0)
                    col = ki * tk + lax.broadcasted_iota(jnp.int32, (tr, tk), 1)
                    mask = row > col
                    log_keep = jnp.where(mask, log_keep, 0.0)
                a = jnp.exp(log_break + _dot01_right(log_keep, u_ref[...]) + keep_sc[hh, rows])
                if masked:
                    a = jnp.where(mask, a, 0.0)
                acc_sc[rows, hc] += _dot(a.astype(BF16), v)
                keep_sc[hh, rows] += jnp.sum(log_keep, axis=-1, keepdims=True)

    on_diagonal = (ki + 1) * tk > qi * tq

    @pl.when(on_diagonal)
    def _():
        update(True)

    @pl.when(jnp.logical_not(on_diagonal))
    def _():
        update(False)

    @pl.when(ki == 0)
    def _():
        o_ref[...] = acc_sc[...].astype(o_ref.dtype)


def _later_matrix(n):
    return (lax.broadcasted_iota(jnp.int32, (n, n), 0) > lax.broadcasted_iota(jnp.int32, (n, n), 1)).astype(BF16)


def sb_attention(q, kv, tq, tk, hps, tr):
    nb, s, _ = q.shape
    tq, tk = min(tq, s), min(tk, s)
    tr = min(tr, tq)
    assert SB_HEADS % hps == 0 and tq % tr == 0
    n_groups = SB_HEADS // hps
    qi, ki = _pairs(s // tq, tq, tk, newest_first=True)
    grid_spec = pltpu.PrefetchScalarGridSpec(
        num_scalar_prefetch=2,
        grid=(nb, n_groups, qi.shape[0]),
        in_specs=[
            pl.BlockSpec((None, tq, hps * SB_HEAD), lambda b, h, p, qi, ki: (b, qi[p], h)),
            pl.BlockSpec((None, tk, hps * SB_HEAD), lambda b, h, p, qi, ki: (b, ki[p], h)),
            pl.BlockSpec((None, tk, hps * SB_HEAD), lambda b, h, p, qi, ki: (b, ki[p], n_groups + h)),
            pl.BlockSpec((tk, tk), lambda b, h, p, qi, ki: (0, 0)),
        ],
        out_specs=pl.BlockSpec((None, tq, hps * SB_HEAD), lambda b, h, p, qi, ki: (b, qi[p], h)),
        scratch_shapes=[pltpu.VMEM((hps, tq, 1), F32), pltpu.VMEM((tq, hps * SB_HEAD), F32)],
    )
    return pl.pallas_call(
        functools.partial(_sb_kernel, tq=tq, tk=tk, hps=hps, tr=tr),
        out_shape=jax.ShapeDtypeStruct((nb, s, SB_HEADS * SB_HEAD), BF16),
        grid_spec=grid_spec,
        compiler_params=_cparams(("arbitrary", "arbitrary", "arbitrary")),
        name="sb_attention",
    )(qi, ki, q, kv, kv, _later_matrix(tk))


def _hgrn_kernel(x_ref, lbl_ref, st0_ref, go_ref, mc_ref, ml_ref, o_ref, stf_ref,
                 st_sc, qd_sc, kd_sc, kl_sc, el_sc, *, layer, chunk, t_valid, ts):
    s = pl.program_id(1)
    hk = HG_HEADS * HG_DK
    cdt = BF16 if chunk % 16 == 0 else F32

    @pl.when(s == 0)
    def _():
        for hd in range(HG_HEADS):
            st_sc[hd] = st0_ref[hd].T

    logits = lbl_ref[...]
    e = jnp.exp(logits - jnp.max(logits, axis=0, keepdims=True))
    sm = e / jnp.sum(e, axis=0, keepdims=True)
    lb = jnp.sum(sm[1:layer + 1], axis=0, keepdims=True)
    q = x_ref[:, :hk]
    forget = lb + (1.0 - lb) * jax.nn.sigmoid(x_ref[:, hk:2 * hk])
    k = 1.0 - forget
    log_f = jnp.log(forget)
    if t_valid < ts:
        valid = lax.broadcasted_iota(jnp.int32, (ts, hk), 0) < t_valid
        log_f = jnp.where(valid, log_f, 0.0)
        k = jnp.where(valid, k, 0.0)
    if cdt == BF16:
        cum = _dot01(mc_ref[...], log_f)
        last = _dot01(ml_ref[...], log_f)
    else:
        cum = jnp.dot(mc_ref[...], log_f, precision=lax.Precision.HIGHEST, preferred_element_type=F32)
        last = jnp.dot(ml_ref[...], log_f, precision=lax.Precision.HIGHEST, preferred_element_type=F32)
    qd_sc[...] = q * jnp.exp(cum)
    kd_sc[...] = k * jnp.exp(-cum)
    kl_sc[...] = k * jnp.exp(last - cum)
    el_sc[...] = jnp.exp(last)
    go = go_ref[...]
    same_chunk_causal = mc_ref[...] > 0

    hv = HG_HEADS * HG_DV
    for hd in range(HG_HEADS):
        cols = slice(HG_DK * hd, HG_DK * (hd + 1))
        vcols = slice(2 * hk + HG_DV * hd, 2 * hk + HG_DV * (hd + 1))
        gcols = slice(2 * hk + hv + HG_DV * hd, 2 * hk + hv + HG_DV * (hd + 1))
        att = jnp.where(same_chunk_causal, _dot_nt(qd_sc[:, cols].astype(cdt), kd_sc[:, cols].astype(cdt)), 0.0)
        intra = _dot(att.astype(cdt), x_ref[:, vcols].astype(cdt))
        st = st_sc[hd]
        for c in range(ts // chunk):
            rows = slice(c * chunk, (c + 1) * chunk)
            qd = qd_sc[rows, cols].astype(cdt)
            kl = kl_sc[rows, cols].astype(cdt)
            v = x_ref[rows, vcols].astype(cdt)
            el = el_sc[c * chunk:c * chunk + 1, cols]
            o = _dot_nt(qd, st.astype(cdt)) + intra[rows]
            st = st * el + _dot_tn(v, kl)
            on = _rms(o, go) * _silu(x_ref[rows, gcols])
            o_ref[rows, HG_DV * hd:HG_DV * (hd + 1)] = on.astype(o_ref.dtype)
        st_sc[hd] = st

    @pl.when(s == pl.num_programs(1) - 1)
    def _():
        for hd in range(HG_HEADS):
            stf_ref[hd] = st_sc[hd].T


def _chunk_matrices(ts, chunk):
    r = lax.broadcasted_iota(jnp.int32, (ts, ts), 0)
    c = lax.broadcasted_iota(jnp.int32, (ts, ts), 1)
    same = (r // chunk) == (c // chunk)
    dtype = BF16 if chunk % 16 == 0 else F32
    return (same & (c <= r)).astype(dtype), same.astype(dtype)


def hgrn_scan(x, lb_logits, state0, g_o, layer, chunk, t_valid, ts, out_dtype):
    nb, s, _ = x.shape
    ts = min(ts, s)
    hk, hv = HG_HEADS * HG_DK, HG_HEADS * HG_DV
    mc, ml = _chunk_matrices(ts, chunk)
    return pl.pallas_call(
        functools.partial(_hgrn_kernel, layer=layer, chunk=chunk, t_valid=t_valid, ts=ts),
        out_shape=[
            jax.ShapeDtypeStruct((nb, s, hv), out_dtype),
            jax.ShapeDtypeStruct((nb, HG_HEADS, HG_DK, HG_DV), F32),
        ],
        grid=(nb, s // ts),
        in_specs=[
            pl.BlockSpec((None, ts, 2 * hk + 2 * hv), lambda b, i: (b, i, 0)),
            pl.BlockSpec(lb_logits.shape, lambda b, i: (0, 0)),
            pl.BlockSpec((None, HG_HEADS, HG_DK, HG_DV), lambda b, i: (b, 0, 0, 0)),
            pl.BlockSpec((1, HG_DV), lambda b, i: (0, 0)),
            pl.BlockSpec((ts, ts), lambda b, i: (0, 0)),
            pl.BlockSpec((ts, ts), lambda b, i: (0, 0)),
        ],
        out_specs=[
            pl.BlockSpec((None, ts, hv), lambda b, i: (b, i, 0)),
            pl.BlockSpec((None, HG_HEADS, HG_DK, HG_DV), lambda b, i: (b, 0, 0, 0)),
        ],
        scratch_shapes=[
            pltpu.VMEM((HG_HEADS, HG_DV, HG_DK), F32),
            pltpu.VMEM((ts, hk), F32),
            pltpu.VMEM((ts, hk), F32),
            pltpu.VMEM((ts, hk), F32),
            pltpu.VMEM((ts, hk), F32),
        ],
        compiler_params=_cparams(("arbitrary", "arbitrary")),
        name="hgrn_scan",
    )(x, lb_logits, state0, g_o.reshape(1, HG_DV), mc, ml)


def _ffn_kernel(*refs, tm, seg, stream):
    if stream:
        (x_ref, mix_ref, wo_ref, gmix_ref, g_ref, sh_ref, sc_ref, gate_ref, wa0_ref, wan_ref, wu_ref, cw_ref,
         cb_ref, wd_ref, y_ref, tail_ref, h_sc, acc_sc, a_sc, xm_sc, carry_sc) = refs
    else:
        (x_ref, mix_ref, wo_ref, gmix_ref, g_ref, sh_ref, sc_ref, gate_ref, wa0_ref, wan_ref, wu_ref, cw_ref,
         cb_ref, wd_ref, past_ref, y_ref, tail_ref, h_sc, acc_sc, a_sc, xm_sc) = refs
    i = pl.program_id(1)
    f = pl.program_id(2)

    @pl.when(f == 0)
    def _():
        xm = x_ref[...] + gmix_ref[...] * _dot(mix_ref[...].astype(BF16), wo_ref[...])
        xm_sc[...] = xm
        h_sc[...] = _norm_mod(xm, g_ref[...], sh_ref[...], sc_ref[...]).astype(BF16)
        acc_sc[...] = jnp.zeros(acc_sc.shape, F32)
        a_sc[0] = _dot(h_sc[...], wa0_ref[...])

    if stream:
        @pl.when(i == 0)
        def _():
            carry_sc[f] = jnp.zeros(carry_sc.shape[1:], F32)

    def step(cur, nxt):
        h = h_sc[...]
        a = a_sc[cur]
        a_sc[nxt] = _dot(h, wan_ref[...])
        u = _dot(h, wu_ref[...])
        tf = a.shape[1]
        row = lax.broadcasted_iota(jnp.int32, (tm, tf), 0)
        a1 = pltpu.roll(a, 1, 0)
        a2 = pltpu.roll(a, 2, 0)
        if stream:
            c = carry_sc[f]
            a1 = jnp.where(row == 0, c[SUBLANES - 1:SUBLANES], a1)
            a2 = jnp.where(row == 0, c[SUBLANES - 2:SUBLANES - 1],
                           jnp.where(row == 1, c[SUBLANES - 1:SUBLANES], a2))
            carry_sc[f] = a[tm - SUBLANES:, :]
            tail_ref[...] = a[tm - SUBLANES:, :]
        else:
            t = jnp.bitwise_and(row, seg - 1)
            a1 = jnp.where(t == 0, past_ref[1], a1)
            a2 = jnp.where(t < 2, past_ref[0], a2)
            tail_ref[...] = a
        cw = cw_ref[...]
        conv = cb_ref[...] + a2 * cw[0:1] + a1 * cw[1:2] + a * cw[2:3]
        mid = (_silu(conv) * u).astype(BF16)
        acc_sc[...] += _dot(mid, wd_ref[...])

    parity = jnp.bitwise_and(f, 1)

    @pl.when(parity == 0)
    def _():
        step(0, 1)

    @pl.when(parity == 1)
    def _():
        step(1, 0)

    @pl.when(f == pl.num_programs(2) - 1)
    def _():
        y_ref[...] = xm_sc[...] + gate_ref[...] * acc_sc[...]


def conv_ffn(x, mix, w_o, gate_mix, g, shift, scale, gate, w_up, conv_w, conv_b, w_down, past, tm, tf):
    nb, s, d = x.shape
    dff = w_down.shape[0]
    dmix = mix.shape[2]
    tm = min(tm, s)
    nf = dff // tf
    stream = past is None
    in_specs = [
        pl.BlockSpec((None, tm, d), lambda b, i, f: (b, i, 0)),
        pl.BlockSpec((None, tm, dmix), lambda b, i, f: (b, i, 0)),
        pl.BlockSpec((dmix, d), lambda b, i, f: (0, 0)),
        _mod_spec(gate_mix, tm),
        pl.BlockSpec((1, d), lambda b, i, f: (0, 0)),
        _mod_spec(shift, tm),
        _mod_spec(scale, tm),
        _mod_spec(gate, tm),
        pl.BlockSpec((d, tf), lambda b, i, f: (0, 0)),
        pl.BlockSpec((d, tf), lambda b, i, f: (0, jnp.minimum(f + 1, nf - 1))),
        pl.BlockSpec((d, tf), lambda b, i, f: (0, nf + f)),
        pl.BlockSpec((CONV_W, tf), lambda b, i, f: (0, f)),
        pl.BlockSpec((1, tf), lambda b, i, f: (0, f)),
        pl.BlockSpec((tf, d), lambda b, i, f: (f, 0)),
    ]
    args = [x, mix, w_o, gate_mix, g.reshape(1, d), shift, scale, gate, w_up, w_up, w_up, conv_w,
            conv_b.reshape(1, dff), w_down]
    scratch = [pltpu.VMEM((tm, d), BF16), pltpu.VMEM((tm, d), F32), pltpu.VMEM((2, tm, tf), F32),
               pltpu.VMEM((tm, d), F32)]
    if stream:
        tail_shape = jax.ShapeDtypeStruct((nb, s // tm, SUBLANES, dff), F32)
        tail_spec = pl.BlockSpec((None, None, SUBLANES, tf), lambda b, i, f: (b, i, 0, f))
        scratch.append(pltpu.VMEM((nf, SUBLANES, tf), F32))
    else:
        in_specs.append(pl.BlockSpec((CONV_W - 1, tm, tf), lambda b, i, f: (0, i, f)))
        args.append(past)
        tail_shape = jax.ShapeDtypeStruct((nb, s, dff), F32)
        tail_spec = pl.BlockSpec((None, tm, tf), lambda b, i, f: (b, i, f))
    return pl.pallas_call(
        functools.partial(_ffn_kernel, tm=tm, seg=T_PAD, stream=stream),
        out_shape=[jax.ShapeDtypeStruct((nb, s, d), F32), tail_shape],
        grid=(nb, s // tm, nf),
        in_specs=in_specs,
        out_specs=[pl.BlockSpec((None, tm, d), lambda b, i, f: (b, i, 0)), tail_spec],
        scratch_shapes=scratch,
        compiler_params=_cparams(("arbitrary", "arbitrary", "arbitrary")),
        name="conv_ffn",
    )(*args)


def _absorb_kernel(q_ref, wk_ref, gk_ref, qa_ref, qp_ref):
    gk = gk_ref[...]
    for hd in range(MLA_HEADS):
        c0 = 2 * LANES * hd
        qn = (q_ref[:, c0:c0 + LANES] * gk[:, :LANES]).astype(BF16)
        qa_ref[:, c0:c0 + 2 * LANES] = _dot_nt(qn, wk_ref[:, LANES * hd:LANES * (hd + 1)]).astype(qa_ref.dtype)
        qp_ref[:, LANES * hd:LANES * (hd + 1)] = (q_ref[:, c0 + LANES:c0 + 2 * LANES] * gk[:, LANES:]).astype(qp_ref.dtype)


def mla_absorb(q, wk, gk):
    m = q.shape[0]
    return pl.pallas_call(
        _absorb_kernel,
        out_shape=[jax.ShapeDtypeStruct((m, MLA_HEADS * KV_LORA), BF16),
                   jax.ShapeDtypeStruct((m, MLA_HEADS * LANES), BF16)],
        name="mla_absorb",
    )(q, wk, gk)


def _vup_kernel(o_ref, wv_ref, out_ref):
    for hd in range(MLA_HEADS):
        lat = o_ref[:, KV_LORA * hd:KV_LORA * (hd + 1)].astype(BF16)
        out_ref[:, V_HEAD * hd:V_HEAD * (hd + 1)] = _dot(lat, wv_ref[:, V_HEAD * hd:V_HEAD * (hd + 1)])


def mla_vup(o_lat, wv):
    m = o_lat.shape[0]
    return pl.pallas_call(
        _vup_kernel,
        out_shape=jax.ShapeDtypeStruct((m, MLA_HEADS * V_HEAD), F32),
        name="mla_vup",
    )(o_lat, wv)


def _mla_dec_kernel(pt_ref, new_ref, *refs, n_slots):
    page_refs = refs[:n_slots]
    qa_ref, qp_ref, wkt_ref, o_ref, m_sc, l_sc, acc_sc = refs[n_slots:]
    s = pl.program_id(1)
    n_rows = T_PAD * MLA_HEADS

    def scores(tile):
        cb = tile[:KV_LORA].astype(BF16)
        kpt = tile[KV_LORA:]
        kt = _dot(wkt_ref[...], cb)
        ss_pe = jnp.sum(kpt * kpt, axis=0, keepdims=True)
        ss = jnp.concatenate(
            [jnp.sum(kt[QK_NOPE * hd:QK_NOPE * (hd + 1)] ** 2, axis=0, keepdims=True) for hd in range(MLA_HEADS)],
            axis=0)
        r = lax.rsqrt((ss + ss_pe) * (1.0 / QK_HEAD) + EPS)
        sc = _dot(qa_ref[...], cb) + _dot(qp_ref[...], kpt.astype(BF16))
        return sc * jnp.concatenate([r] * T_PAD, axis=0), cb

    def update(scs, cbs):
        m_prev = m_sc[...]
        m_new = m_prev
        for sc in scs:
            m_new = jnp.maximum(m_new, jnp.max(sc, axis=-1, keepdims=True))
        alpha = jnp.exp(m_prev - m_new)
        l_new = alpha * l_sc[...]
        acc = alpha * acc_sc[...]
        for sc, cb in zip(scs, cbs):
            pr = jnp.exp(sc - m_new)
            l_new = l_new + jnp.sum(pr, axis=-1, keepdims=True)
            acc = acc + _dot_nt(pr.astype(BF16), cb)
        l_sc[...] = l_new
        acc_sc[...] = acc
        m_sc[...] = m_new

    @pl.when(s == 0)
    def _():
        m_sc[...] = jnp.full((n_rows, 1), NEG_BIG, F32)
        l_sc[...] = jnp.zeros((n_rows, 1), F32)
        acc_sc[...] = jnp.zeros((n_rows, KV_LORA), F32)
        sc, cb = scores(new_ref[...])
        tok = lax.shift_right_logical(lax.broadcasted_iota(jnp.int32, sc.shape, 0), HEAD_SHIFT)
        key = lax.broadcasted_iota(jnp.int32, sc.shape, 1)
        update([jnp.where(key <= tok, sc, NEG_BIG)], [cb])

    @pl.when(s > 0)
    def _():
        pairs = [scores(jnp.concatenate([page_refs[a][...], page_refs[a + 1][...]], axis=1))
                 for a in range(0, n_slots, 2)]
        update([p[0] for p in pairs], [p[1] for p in pairs])

    @pl.when(s == pl.num_programs(1) - 1)
    def _():
        o_ref[...] = acc_sc[...] / l_sc[...]


def mla_decode(page_table, new_page, cache_t, layer, qa, qp, wk_t, n_slots):
    db, n_pages = page_table.shape
    n_rows = T_PAD * MLA_HEADS
    row_w = cache_t.shape[2]
    assert n_slots % 2 == 0 and n_pages % n_slots == 0
    steps = n_pages // n_slots

    def page_spec(slot):
        def imap(b, s, pt):
            idx = jnp.maximum(s - 1, 0) * n_slots + slot
            return (layer, pt[b * n_pages + idx], 0, 0)
        return pl.BlockSpec((None, None, row_w, PAGE_SIZE), imap)

    grid_spec = pltpu.PrefetchScalarGridSpec(
        num_scalar_prefetch=1,
        grid=(db, steps + 1),
        in_specs=[pl.BlockSpec((None, row_w, PAGE_SIZE), lambda b, s, pt: (b, 0, 0))]
        + [page_spec(i) for i in range(n_slots)]
        + [
            pl.BlockSpec((None, n_rows, KV_LORA), lambda b, s, pt: (b, 0, 0)),
            pl.BlockSpec((None, n_rows, QK_ROPE), lambda b, s, pt: (b, 0, 0)),
            pl.BlockSpec(wk_t.shape, lambda b, s, pt: (0, 0)),
        ],
        out_specs=pl.BlockSpec((None, n_rows, KV_LORA), lambda b, s, pt: (b, 0, 0)),
        scratch_shapes=[pltpu.VMEM((n_rows, 1), F32), pltpu.VMEM((n_rows, 1), F32),
                        pltpu.VMEM((n_rows, KV_LORA), F32)],
    )
    return pl.pallas_call(
        functools.partial(_mla_dec_kernel, n_slots=n_slots),
        out_shape=jax.ShapeDtypeStruct((db, n_rows, KV_LORA), F32),
        grid_spec=grid_spec,
        compiler_params=_cparams(("arbitrary", "arbitrary")),
        name="mla_decode",
    )(page_table.reshape(-1), new_page, *([cache_t] * n_slots), qa, qp, wk_t)


def _sb_dec_kernel(pt_ref, new_ref, *refs, n_slots):
    page_refs = refs[:n_slots]
    q_ref, u_ref, un_ref, shn_ref, o_ref, keep_sc, acc_sc, stage_sc = refs[n_slots:]
    s = pl.program_id(1)
    n_rows = T_PAD * SB_HEADS

    def process(pages, n_keys, causal, cdt, keep, acc, later_of, shift_of, stage):
        w = 2 * n_keys
        hw = SB_HEADS * SB_HEAD
        qt = q_ref[...] * SB_SCALE
        lane_head = lax.shift_right_logical(lax.broadcasted_iota(jnp.int32, (T_PAD, hw), 1), LANE_SHIFT)
        qbd = jnp.concatenate([jnp.where(lane_head == hd, qt, 0.0) for hd in range(SB_HEADS)], axis=0).astype(cdt)
        kvs = []
        for p, ref in enumerate(pages):
            kv = jnp.concatenate([ref[pl.ds(hd, w, stride=SB_HEADS), :] for hd in range(SB_HEADS)], axis=1)
            if stage is None:
                kvs.append(kv.astype(cdt))
            else:
                stage[p] = kv.astype(cdt)
                kvs.append(stage.at[p])
        z = jnp.concatenate([_dot_nt(qbd, kv[...]) for kv in kvs], axis=1)
        log_keep, log_break = _sb_terms(z)
        lane = lax.broadcasted_iota(jnp.int32, z.shape, 1)
        mask = jnp.bitwise_and(lane, 1) == 0
        if causal:
            tok = jnp.bitwise_and(lax.broadcasted_iota(jnp.int32, z.shape, 0), T_PAD - 1)
            mask = mask & (lax.shift_right_logical(lane, 1) < tok)
        log_keep = jnp.where(mask, log_keep, 0.0)
        later = []
        for p in range(len(pages)):
            lk = log_keep[:, w * p:w * (p + 1)]
            later.append(later_of(lk) + keep)
            keep = keep + jnp.sum(lk, axis=-1, keepdims=True)
        a = jnp.where(mask, jnp.exp(log_break + jnp.concatenate(later, axis=1)), 0.0)
        a = shift_of(a)
        full = None
        for p, kv in enumerate(kvs):
            d = _dot(a[:, w * p:w * (p + 1)].astype(cdt), kv[...])
            full = d if full is None else full + d
        own = [full[T_PAD * hd:T_PAD * (hd + 1), SB_HEAD * hd:SB_HEAD * (hd + 1)] for hd in range(SB_HEADS)]
        return keep, acc + jnp.concatenate(own, axis=0)

    @pl.when(s == 0)
    def _():
        exact = functools.partial(jnp.dot, precision=lax.Precision.HIGHEST, preferred_element_type=F32)
        keep, acc = process([new_ref], T_PAD, True, F32, jnp.zeros((n_rows, 1), F32),
                            jnp.zeros((n_rows, SB_HEAD), F32),
                            lambda lk: exact(lk, un_ref[...]), lambda a: exact(a, shn_ref[...]), None)
        keep_sc[...] = keep
        acc_sc[...] = acc

    @pl.when(s > 0)
    def _():
        keep, acc = process(page_refs, PAGE_SIZE, False, BF16, keep_sc[...], acc_sc[...],
                            lambda lk: _dot01_right(lk, u_ref[...]),
                            lambda a: pltpu.roll(a, 1, 1), stage_sc)
        keep_sc[...] = keep
        acc_sc[...] = acc

    @pl.when(s == pl.num_programs(1) - 1)
    def _():
        for hd in range(SB_HEADS):
            o_ref[:, SB_HEAD * hd:SB_HEAD * (hd + 1)] = acc_sc[T_PAD * hd:T_PAD * (hd + 1), :]


def sb_decode(page_table, new_page, cache_rows, layer, q, n_slots):
    db, n_pages = page_table.shape
    n_rows = T_PAD * SB_HEADS
    hw = SB_HEADS * SB_HEAD
    page_rows = cache_rows.shape[2]
    assert n_pages % n_slots == 0
    steps = n_pages // n_slots

    def page_spec(slot):
        def imap(b, s, pt):
            idx = n_pages - 1 - (jnp.maximum(s - 1, 0) * n_slots + slot)
            return (layer, pt[b * n_pages + idx], 0, 0)
        return pl.BlockSpec((None, None, page_rows, SB_HEAD), imap)

    grid_spec = pltpu.PrefetchScalarGridSpec(
        num_scalar_prefetch=1,
        grid=(db, steps + 1),
        in_specs=[pl.BlockSpec((None,) + new_page.shape[1:], lambda b, s, pt: (b, 0, 0))]
        + [page_spec(i) for i in range(n_slots)]
        + [pl.BlockSpec((None, T_PAD, hw), lambda b, s, pt: (b, 0, 0))]
        + [pl.BlockSpec((2 * n, 2 * n), lambda b, s, pt: (0, 0)) for n in (PAGE_SIZE, T_PAD, T_PAD)],
        out_specs=pl.BlockSpec((None, T_PAD, hw), lambda b, s, pt: (b, 0, 0)),
        scratch_shapes=[pltpu.VMEM((n_rows, 1), F32), pltpu.VMEM((n_rows, SB_HEAD), F32),
                        pltpu.VMEM((n_slots, 2 * PAGE_SIZE, hw), BF16)],
    )
    return pl.pallas_call(
        functools.partial(_sb_dec_kernel, n_slots=n_slots),
        out_shape=jax.ShapeDtypeStruct((db, T_PAD, hw), F32),
        grid_spec=grid_spec,
        compiler_params=_cparams(("arbitrary", "arbitrary")),
        name="sb_decode",
    )(page_table.reshape(-1), new_page, *([cache_rows] * n_slots), q,
      _interleaved_matrices(PAGE_SIZE, BF16)[0], *_interleaved_matrices(T_PAD, F32))


def _interleaved_matrices(n_keys, dtype):
    r = lax.broadcasted_iota(jnp.int32, (2 * n_keys, 2 * n_keys), 0)
    c = lax.broadcasted_iota(jnp.int32, (2 * n_keys, 2 * n_keys), 1)
    return ((r // 2) > (c // 2)).astype(dtype), (c == r + 1).astype(dtype)


TM_LINEAR = 512
TM_FFN = 1024
TF_FFN = 256
TQ_MLA, TK_MLA = 512, 512
TQ_SB, TK_SB = 512, 256
MLA_HEADS_PER_STEP, MLA_ROW_BLOCK = 4, 256
SB_HEADS_PER_STEP, SB_ROW_BLOCK = 4, 512
TS_HGRN = 256
MLA_PAGES_PER_STEP = 16
SB_PAGES_PER_STEP = 8


def _pad_tokens(x, t_pad):
    return jnp.pad(x, ((0, 0), (0, t_pad - x.shape[1])) + ((0, 0),) * (x.ndim - 2))


def kernel(x_prompt, x_sample, c_prompt, c_sample, page_table, cache_mla, cache_sb_kv, state_hgrn,
           state_ffn_conv, w_mod, b_mod, norm_g, w_mla_a, g_mla_qa, g_mla_kva, w_mla_uq, w_mla_ukv,
           g_mla_q, g_mla_k, w_mla_o, w_sb_qkv, w_sb_o, w_hg_in, hg_lb_logits, g_hg_o, w_hg_o,
           w_ffn_up, ffn_conv_w, ffn_conv_b, w_ffn_down):
    b, s_len, d = x_prompt.shape
    db, t, _ = x_sample.shape
    depth = w_mod.shape[0]
    n_pages = page_table.shape[1]
    past = n_pages * PAGE_SIZE
    dff = w_ffn_down.shape[1]
    assert t <= T_PAD and t >= CONV_W - 1

    mod = ada_mod(jnp.concatenate([c_prompt, c_sample], axis=0), w_mod, b_mod)
    mod = mod.reshape(depth, b + db, N_MOD, d)
    cos_p, sin_p = _rope_tables(jnp.arange(s_len))
    cos_s, sin_s = _rope_tables(jnp.tile(past + jnp.arange(T_PAD), db))
    sb_cache = cache_sb_kv.reshape(cache_sb_kv.shape[:2] + (PAGE_SIZE * 2 * SB_HEADS, SB_HEAD))
    mla_cache_t = jnp.swapaxes(cache_mla, 2, 3)

    xp = x_prompt
    xs = _pad_tokens(x_sample, T_PAD).reshape(1, db * T_PAD, d)
    mla_p, mla_s, sb_p, sb_s, hg_p, hg_s, ffn_p, ffn_s = [], [], [], [], [], [], [], []
    for i in range(depth):
        mp = [mod[i, :b, j].reshape(b, 1, d) for j in range(N_MOD)]
        ms = [jnp.repeat(mod[i, b:, j], T_PAD, axis=0).reshape(1, db * T_PAD, d) for j in range(N_MOD)]
        kind, j = i % N_MIXERS, i // N_MIXERS
        if kind == 0:
            wp = _mla_weights(w_mla_a[j], g_mla_qa[j], g_mla_kva[j], w_mla_uq[j], w_mla_ukv[j],
                              g_mla_q[j], g_mla_k[j])
            rows_p, qp, kp, vp = mla_proj(xp, norm_g[i, 0], mp[0], mp[1], cos_p, sin_p, wp, BF16, TM_LINEAR)
            op = flash_mla(qp, kp, vp, TQ_MLA, TK_MLA, MLA_HEADS_PER_STEP, MLA_ROW_BLOCK)
            rows_s, qs, _, _ = mla_proj(xs, norm_g[i, 0], ms[0], ms[1], cos_s, sin_s, wp, F32, TM_LINEAR)
            qa, qpe = mla_absorb(qs[0], wp["wk"], wp["gk"])
            qa = qa.reshape(db, T_PAD * MLA_HEADS, KV_LORA)
            qpe = qpe.reshape(db, T_PAD * MLA_HEADS, LANES)[:, :, :QK_ROPE]
            rows_s = rows_s.reshape(db, T_PAD, -1)
            new_page = jnp.swapaxes(_pad_tokens(rows_s, PAGE_SIZE), 1, 2)
            o_lat = mla_decode(page_table, new_page, mla_cache_t, j, qa, qpe, wp["wk_t"], MLA_PAGES_PER_STEP)
            os_ = mla_vup(o_lat.reshape(db * T_PAD, MLA_HEADS * KV_LORA), wp["wv"])
            os_ = os_.reshape(1, db * T_PAD, -1)
            w_o = w_mla_o[j]
            mla_p.append(rows_p)
            mla_s.append(rows_s[:, :t])
        elif kind == 1:
            wq = w_sb_qkv[j].astype(BF16)
            hw = SB_HEADS * SB_HEAD
            q_p, kv_p = norm_mod_linear(xp, norm_g[i, 0], mp[0], mp[1], [wq[:, :hw], wq[:, hw:]], TM_LINEAR)
            op = sb_attention(q_p, kv_p, TQ_SB, TK_SB, SB_HEADS_PER_STEP, SB_ROW_BLOCK)
            q_s, kv_s = norm_mod_linear(xs, norm_g[i, 0], ms[0], ms[1], [wq[:, :hw], wq[:, hw:]], TM_LINEAR)
            kv_s = kv_s.reshape(db, T_PAD, 2 * hw)
            new_page = kv_s.reshape(db, T_PAD * 2 * SB_HEADS, SB_HEAD)
            os_ = sb_decode(page_table, new_page, sb_cache, j, q_s.reshape(db, T_PAD, hw), SB_PAGES_PER_STEP)
            os_ = os_.reshape(1, db * T_PAD, hw)
            w_o = w_sb_o[j]
            sb_p.append(kv_p.reshape(b, s_len, 2, SB_HEADS, SB_HEAD))
            sb_s.append(kv_s[:, :t].reshape(db, t, 2, SB_HEADS, SB_HEAD))
        else:
            w_in = w_hg_in[j].astype(BF16)
            (zp,) = norm_mod_linear(xp, norm_g[i, 0], mp[0], mp[1], [w_in], TM_LINEAR)
            zero_state = jnp.zeros((b, HG_HEADS, HG_DK, HG_DV), F32)
            op, st_p = hgrn_scan(zp, hg_lb_logits, zero_state, g_hg_o[j], i, HG_CHUNK, s_len, TS_HGRN, BF16)
            (zs,) = norm_mod_linear(xs, norm_g[i, 0], ms[0], ms[1], [w_in], TM_LINEAR)
            os_, st_s = hgrn_scan(zs.reshape(db, T_PAD, -1), hg_lb_logits, state_hgrn[j], g_hg_o[j], i,
                                  T_PAD, t, T_PAD, F32)
            os_ = os_.reshape(1, db * T_PAD, -1)
            w_o = w_hg_o[j]
            hg_p.append(st_p)
            hg_s.append(st_s)
        w_o = w_o.astype(BF16)
        w_up = w_ffn_up[i].astype(BF16)
        w_down = w_ffn_down[i].astype(BF16)
        xp, tail_p = conv_ffn(xp, op, w_o, mp[2], norm_g[i, 1], mp[3], mp[4], mp[5], w_up, ffn_conv_w[i],
                              ffn_conv_b[i], w_down, None, TM_FFN, TF_FFN)
        st = state_ffn_conv[i]
        z = jnp.zeros((db, T_PAD - 2, dff), F32)
        past2 = jnp.concatenate([st, z], axis=1)
        past1 = jnp.concatenate([st[:, 1:], z, z[:, :1]], axis=1)
        past_rows = jnp.stack([past2, past1]).reshape(CONV_W - 1, db * T_PAD, dff)
        xs, a_s = conv_ffn(xs, os_, w_o, ms[2], norm_g[i, 1], ms[3], ms[4], ms[5], w_up, ffn_conv_w[i],
                           ffn_conv_b[i], w_down, past_rows, TM_FFN, TF_FFN)
        ffn_p.append(tail_p[:, -1, SUBLANES - (CONV_W - 1):])
        ffn_s.append(a_s.reshape(db, T_PAD, dff)[:, t - (CONV_W - 1):t])
    ys = xs.reshape(db, T_PAD, d)[:, :t]
    return (xp, ys, jnp.stack(mla_p), jnp.stack(mla_s), jnp.stack(sb_p), jnp.stack(sb_s),
            jnp.stack(hg_p), jnp.stack(hg_s), jnp.stack(ffn_p), jnp.stack(ffn_s))
```
